```python
import math
import jax
import jax.numpy as jnp
from jax import lax
import numpy as np

D_MODEL = 4096
BATCH = 2
SEQ = 8192
DEPTH = 2

GRID_W = 64
CTX_LEN = 256
HEAD_DIM = 128
MIX_W = D_MODEL
A_HEADS = (MIX_W // 2) // HEAD_DIM
A_KV_HEADS = A_HEADS // 4
A_GROUP = A_HEADS // A_KV_HEADS
DN_HEADS = (MIX_W // 2) // HEAD_DIM
DN_W = DN_HEADS * HEAD_DIM
DN_CHUNK = 64
CONV_W = 4
Q_BLOCK = 128
ROPE_THETA = 10000.0
AX_DIM = HEAD_DIM // 2
POOL_WINDOWS = (2, 4, 8, 16)
POOL_GROUPS = len(POOL_WINDOWS)
POOL_W = D_MODEL // POOL_GROUPS
FFN_HIDDEN = ((8 * D_MODEL // 3 + 255) // 256) * 256
N_EVEN = (DEPTH + 1) // 2
N_ODD = DEPTH // 2
EPS = 1e-6
PROJ_SPLITS = (A_HEADS * HEAD_DIM, A_KV_HEADS * HEAD_DIM, A_KV_HEADS * HEAD_DIM,
               DN_W, DN_W, DN_W, DN_W, 2 * DN_HEADS, 2 * DN_HEADS)
PROJ_W = sum(PROJ_SPLITS)

kernel_name = "hybrid_gqa_gdn_pool_dit_block"


def rmsnorm(x, g):
    xf = x.astype(jnp.float32)
    y = xf * lax.rsqrt(jnp.mean(xf * xf, axis=-1, keepdims=True) + EPS) * g.astype(jnp.float32)
    return y.astype(x.dtype)


def modulate(x, g, shift, scale):
    return rmsnorm(x, g) * (1 + scale) + shift


def l2norm(x):
    return x * lax.rsqrt(jnp.sum(x * x, axis=-1, keepdims=True) + EPS)


def swiglu(h, wg, wu, wd):
    return (jax.nn.silu(h @ wg) * (h @ wu)) @ wd


def split_projection(p):
    idx = [int(v) for v in np.cumsum(PROJ_SPLITS)[:-1]]
    return jnp.split(p, idx, axis=-1)


def axial_rope_tables(rows_idx, cols_idx):
    inv = jnp.float32(ROPE_THETA) ** (-jnp.arange(0, AX_DIM, 2, dtype=jnp.float32) / AX_DIM)
    ang = jnp.concatenate([rows_idx.astype(jnp.float32)[:, None] * inv,
                           cols_idx.astype(jnp.float32)[:, None] * inv], axis=-1)
    return jnp.cos(ang), jnp.sin(ang)


def apply_rope(x, cos, sin):
    xf = x.astype(jnp.float32).reshape(x.shape[:-1] + (x.shape[-1] // 2, 2))
    x0, x1 = xf[..., 0], xf[..., 1]
    shape = (1, cos.shape[0]) + (1,) * (x.ndim - 3) + (cos.shape[1],)
    cs, sn = cos.reshape(shape), sin.reshape(shape)
    out = jnp.stack([x0 * cs - x1 * sn, x0 * sn + x1 * cs], axis=-1).reshape(x.shape)
    return out.astype(x.dtype)


def attn_heads(aq, ak, av, q_norm, k_norm):
    b, n, _ = aq.shape
    q = rmsnorm(aq.reshape(b, n, A_KV_HEADS, A_GROUP, HEAD_DIM), q_norm)
    k = rmsnorm(ak.reshape(b, n, A_KV_HEADS, HEAD_DIM), k_norm)
    v = av.reshape(b, n, A_KV_HEADS, HEAD_DIM)
    return q, k, v


def block_attention(q, k, v):
    b, n, kvh, grp, hd = q.shape
    nb = n // Q_BLOCK
    qb = jnp.moveaxis(q.reshape(b, nb, Q_BLOCK, kvh, grp, hd), 1, 0)
    scale = hd ** -0.5

    def one_block(qblk):
        s = jnp.einsum('bqhgd,bkhd->bhgqk', qblk, k, preferred_element_type=jnp.float32) * scale
        p = jax.nn.softmax(s, axis=-1).astype(v.dtype)
        return jnp.einsum('bhgqk,bkhd->bqhgd', p, v)

    o = lax.map(one_block, qb)
    return jnp.moveaxis(o, 0, 1).reshape(b, n, kvh * grp * hd)


def short_conv(x, w):
    n = x.shape[1]
    left = CONV_W // 2
    xp = jnp.pad(x, ((0, 0), (left, CONV_W - 1 - left), (0, 0)))
    y = xp[:, 0:n] * w[0]
    for j in range(1, CONV_W):
        y = y + xp[:, j:j + n] * w[j]
    return y


def delta_inputs(dq, dk, dv, da, db, conv_w, a_log, dt_bias):
    b, n, _ = dq.shape
    qkv = jax.nn.silu(short_conv(jnp.concatenate([dq, dk, dv], axis=-1), conv_w))
    q, k, v = jnp.split(qkv, 3, axis=-1)

    def heads(t):
        return t.reshape(b, n, DN_HEADS, HEAD_DIM).transpose(0, 2, 1, 3).astype(jnp.float32)

    q = l2norm(heads(q)) * (HEAD_DIM ** -0.5)
    k = l2norm(heads(k))
    v = heads(v)
    a = da.astype(jnp.float32).reshape(b, n, 2, DN_HEADS)
    g = -jnp.exp(a_log.astype(jnp.float32)) * jax.nn.softplus(a + dt_bias.astype(jnp.float32))
    beta = jax.nn.sigmoid(db.astype(jnp.float32).reshape(b, n, 2, DN_HEADS))
    return q, k, v, g.transpose(2, 0, 3, 1), beta.transpose(2, 0, 3, 1)


def chunk_gated_delta(q, k, v, g, beta, s0):
    b, h, n, _ = q.shape
    dv = v.shape[-1]
    nc = n // DN_CHUNK

    def rs(t):
        return t.reshape((b, h, nc, DN_CHUNK) + t.shape[3:])

    q, k, v, g, beta = rs(q), rs(k), rs(v), rs(g), rs(beta)
    g = jnp.cumsum(g, axis=-1)
    idx = jnp.arange(DN_CHUNK)
    incl = idx[:, None] >= idx[None, :]
    strict = idx[:, None] > idx[None, :]
    decay = jnp.exp(jnp.where(incl, g[..., :, None] - g[..., None, :], -jnp.inf))
    kb = k * beta[..., None]
    lmat = jnp.where(strict, jnp.einsum('bhncd,bhnsd->bhncs', kb, k) * decay, 0.0)
    rhs = jnp.concatenate([v * beta[..., None], kb * jnp.exp(g)[..., None]], axis=-1)
    sol = lax.linalg.triangular_solve(lmat + jnp.eye(DN_CHUNK, dtype=lmat.dtype), rhs,
                                      left_side=True, lower=True, unit_diagonal=True)
    u, w = sol[..., :dv], sol[..., dv:]
    qk = jnp.einsum('bhncd,bhnsd->bhncs', q, k) * decay
    q_dec = q * jnp.exp(g)[..., None]
    k_dec = k * jnp.exp(g[..., -1:] - g)[..., None]
    g_tot = jnp.exp(g[..., -1])
    xs = tuple(jnp.moveaxis(t, 2, 0) for t in (u, w, qk, q_dec, k_dec, g_tot))

    def step(s, inp):
        u_c, w_c, qk_c, qd_c, kd_c, gt_c = inp
        v_new = u_c - jnp.einsum('bhck,bhkv->bhcv', w_c, s)
        o = jnp.einsum('bhck,bhkv->bhcv', qd_c, s) + jnp.einsum('bhcs,bhsv->bhcv', qk_c, v_new)
        s = s * gt_c[..., None, None] + jnp.einsum('bhck,bhcv->bhkv', kd_c, v_new)
        return s, o

    s, o = lax.scan(step, s0, xs)
    return jnp.moveaxis(o, 0, 2).reshape(b, h, n, dv), s


def bidirectional_delta(lat, ctx):
    ql, kl, vl, gl, bl = lat
    qc, kc, vc, gc, bc = ctx
    b, h, _, dk = ql.shape
    s0 = jnp.zeros((b, h, dk, vl.shape[-1]), jnp.float32)

    def flip(t):
        return jnp.flip(t, axis=2)

    oc_f, sc_f = chunk_gated_delta(qc, kc, vc, gc[0], bc[0], s0)
    ol_f, _ = chunk_gated_delta(ql, kl, vl, gl[0], bl[0], sc_f)
    oc_b, sc_b = chunk_gated_delta(flip(qc), flip(kc), flip(vc), flip(gc[1]), flip(bc[1]), s0)
    ol_b, _ = chunk_gated_delta(flip(ql), flip(kl), flip(vl), flip(gl[1]), flip(bl[1]), sc_b)
    return ol_f + flip(ol_b), oc_f + flip(oc_b)


def gated_rmsnorm(o, z, w):
    b, hh, n, dv = o.shape
    of = o.transpose(0, 2, 1, 3)
    zf = z.astype(jnp.float32).reshape(b, n, hh, dv)
    y = of * lax.rsqrt(jnp.mean(of * of, axis=-1, keepdims=True) + EPS) * w.astype(jnp.float32) * jax.nn.silu(zf)
    return y.reshape(b, n, hh * dv).astype(z.dtype)


def hybrid_mixer(h, hc, w_in, w_out, q_norm, k_norm, conv_w, a_log, dt_bias, o_norm, ctx_out):
    b, n, _ = h.shape
    rows = n // GRID_W
    r_idx = jnp.repeat(jnp.arange(rows), GRID_W)
    c_idx = jnp.tile(jnp.arange(GRID_W), rows)
    cos, sin = axial_rope_tables(r_idx, c_idx)

    pl = split_projection(h @ w_in)
    pc = split_projection(hc @ w_in)

    qa, ka, va = attn_heads(pl[0], pl[1], pl[2], q_norm, k_norm)
    qa, ka = apply_rope(qa, cos, sin), apply_rope(ka, cos, sin)
    qac, kac, vac = attn_heads(pc[0], pc[1], pc[2], q_norm, k_norm)
    k_all = jnp.concatenate([ka, kac], axis=1)
    v_all = jnp.concatenate([va, vac], axis=1)
    ya = block_attention(qa, k_all, v_all)

    dl = delta_inputs(pl[3], pl[4], pl[5], pl[7], pl[8], conv_w, a_log, dt_bias)
    dc = delta_inputs(pc[3], pc[4], pc[5], pc[7], pc[8], conv_w, a_log, dt_bias)
    ob, obc = bidirectional_delta(dl, dc)
    yb = gated_rmsnorm(ob, pl[6], o_norm)

    y = jnp.concatenate([ya, yb], axis=-1) @ w_out
    if not ctx_out:
        return y, None
    yac = block_attention(qac, kac, vac)
    ybc = gated_rmsnorm(obc, pc[6], o_norm)
    return y, jnp.concatenate([yac, ybc], axis=-1) @ w_out


def pool_mixer(h, w_pool, pool_scale):
    b, n, d = h.shape
    hf = h.astype(jnp.float32)
    csum = jnp.concatenate([jnp.zeros((b, 1, d), jnp.float32), jnp.cumsum(hf, axis=1)], axis=1)
    t = jnp.arange(n)
    outs = []
    for gi, win in enumerate(POOL_WINDOWS):
        left = win // 2
        right = win - 1 - left
        hi = jnp.minimum(t + right + 1, n)
        lo = jnp.maximum(t - left, 0)
        cg = csum[:, :, gi * POOL_W:(gi + 1) * POOL_W]
        mean = (jnp.take(cg, hi, axis=1) - jnp.take(cg, lo, axis=1)) / (hi - lo).astype(jnp.float32)[None, :, None]
        pooled = (mean - hf[:, :, gi * POOL_W:(gi + 1) * POOL_W]).astype(h.dtype)
        outs.append(pooled @ w_pool[gi])
    return jnp.concatenate(outs, axis=-1) * pool_scale


def setup_inputs(seed: int = 0) -> dict:
    key = jax.random.key(seed)
    ks = jax.random.split(key, 24)
    D = D_MODEL
    f32 = jnp.float32

    def nrm(k, shape, scale):
        return jax.random.normal(k, shape, f32) * scale

    dt = jnp.exp(jax.random.uniform(ks[14], (N_EVEN, 2, DN_HEADS), f32, math.log(1e-3), math.log(1e-1)))
    return {
        "x": nrm(ks[0], (BATCH, SEQ, D), 1.0),
        "c": nrm(ks[1], (BATCH, D), 1.0),
        "ctx": nrm(ks[2], (BATCH, CTX_LEN, D), 1.0),
        "c_ctx": nrm(ks[3], (D,), 1.0),
        "w_mod": nrm(ks[4], (DEPTH, D, 6 * D), 0.5 * D ** -0.5),
        "b_mod": nrm(ks[5], (DEPTH, 6 * D), 0.02),
        "norm_mix": 1.0 + nrm(ks[6], (DEPTH, D), 0.05),
        "norm_ffn": 1.0 + nrm(ks[7], (DEPTH, D), 0.05),
        "w_in": nrm(ks[8], (N_EVEN, D, PROJ_W), D ** -0.5),
        "w_out": nrm(ks[9], (N_EVEN, MIX_W, D), MIX_W ** -0.5),
        "q_norm": 1.0 + nrm(ks[10], (N_EVEN, HEAD_DIM), 0.05),
        "k_norm": 1.0 + nrm(ks[11], (N_EVEN, HEAD_DIM), 0.05),
        "conv_w": nrm(ks[12], (N_EVEN, CONV_W, 3 * DN_W), CONV_W ** -0.5),
        "a_log": jnp.log(jax.random.uniform(ks[13], (N_EVEN, 2, DN_HEADS), f32, 1.0, 16.0)),
        "dt_bias": dt + jnp.log(-jnp.expm1(-dt)),
        "o_norm": 1.0 + nrm(ks[15], (N_EVEN, HEAD_DIM), 0.05),
        "w_pool": nrm(ks[16], (N_ODD, POOL_GROUPS, POOL_W, POOL_W), POOL_W ** -0.5),
        "pool_scale": 1.0 + nrm(ks[17], (N_ODD, D), 0.1),
        "w_gate": nrm(ks[18], (DEPTH, D, FFN_HIDDEN), D ** -0.5),
        "w_up": nrm(ks[19], (DEPTH, D, FFN_HIDDEN), D ** -0.5),
        "w_down": nrm(ks[20], (DEPTH, FFN_HIDDEN, D), FFN_HIDDEN ** -0.5),
    }


def reference(x, c, ctx, c_ctx, w_mod, b_mod, norm_mix, norm_ffn, w_in, w_out, q_norm, k_norm,
              conv_w, a_log, dt_bias, o_norm, w_pool, pool_scale, w_gate, w_up, w_down):
    xc = ctx
    for i in range(DEPTH):
        is_even = i % 2 == 0
        ctx_carry = any(j % 2 == 0 for j in range(i + 1, DEPTH))
        mod = jax.nn.silu(c) @ w_mod[i] + b_mod[i]
        sh1, sc1, g1, sh2, sc2, g2 = jnp.split(mod[:, None, :], 6, axis=-1)
        h = modulate(x, norm_mix[i], sh1, sc1)
        if is_even or ctx_carry:
            mod_c = jax.nn.silu(c_ctx) @ w_mod[i] + b_mod[i]
            csh1, csc1, cg1, csh2, csc2, cg2 = jnp.split(mod_c[None, None, :], 6, axis=-1)
            hc = modulate(xc, norm_mix[i], csh1, csc1)
        if is_even:
            e = i // 2
            m, mc = hybrid_mixer(h, hc, w_in[e], w_out[e], q_norm[e], k_norm[e], conv_w[e],
                                 a_log[e], dt_bias[e], o_norm[e], ctx_carry)
        else:
            o = i // 2
            m = pool_mixer(h, w_pool[o], pool_scale[o])
            mc = pool_mixer(hc, w_pool[o], pool_scale[o]) if ctx_carry else None
        x = x + g1 * m
        x = x + g2 * swiglu(modulate(x, norm_ffn[i], sh2, sc2), w_gate[i], w_up[i], w_down[i])
        if ctx_carry:
            xc = xc + cg1 * mc
            xc = xc + cg2 * swiglu(modulate(xc, norm_ffn[i], csh2, csc2), w_gate[i], w_up[i], w_down[i])
    return x
```

```python
import functools

import jax
import jax.numpy as jnp
from jax import lax
from jax.experimental import pallas as pl
from jax.experimental.pallas import tpu as pltpu

F32 = jnp.float32
MXU_DTYPE = jnp.bfloat16

EPS = 1e-6
HEAD_DIM = 128
GQA_GROUP = 4
GRID_W = 64
ROPE_THETA = 10000.0
CONV_TAPS = (-2, -1, 0, 1)
DN_CHUNK = 64
DN_GROUP = 256
DN_DIAG = 16
POOL_WINDOWS = (2, 4, 8, 16)
SEQ_PAD = 8
FFN_ALIGN = 1024
V7X_VMEM_LIMIT = 56 * 1024 * 1024


def _tile(dim, pref, mult=128):
    if dim <= pref:
        return dim
    t = (pref // mult) * mult
    while t >= mult:
        if dim % t == 0:
            return t
        t -= mult
    raise ValueError(f"no tile for {dim} (pref {pref}, mult {mult})")


def _cparams(*sem):
    return pltpu.CompilerParams(dimension_semantics=sem, vmem_limit_bytes=V7X_VMEM_LIMIT)


def _dot(a, b):
    return jnp.dot(a, b, preferred_element_type=F32)


def _dot_nt(a, b):
    return lax.dot_general(a, b, (((1,), (1,)), ((), ())), preferred_element_type=F32)


def _split2(a):
    hi = a.astype(MXU_DTYPE)
    lo = (a - hi.astype(F32)).astype(MXU_DTYPE)
    return hi, lo


def _dot3(a, b):
    ah, al = _split2(a)
    bh, bl = _split2(b)
    return _dot(ah, bh) + _dot(ah, bl) + _dot(al, bh)


def _dot_exact_lhs(m, x):
    x1 = x.astype(MXU_DTYPE)
    r1 = x - x1.astype(F32)
    x2 = r1.astype(MXU_DTYPE)
    x3 = (r1 - x2.astype(F32)).astype(MXU_DTYPE)
    return _dot(m, x1) + _dot(m, x2) + _dot(m, x3)


def _mod_kernel(s_ref, w_ref, b_ref, o_ref):
    s = s_ref[...]
    s = s * jax.nn.sigmoid(s)
    o_ref[...] = _dot(s.astype(MXU_DTYPE), w_ref[...].astype(MXU_DTYPE)) + b_ref[...]


def _mod_vectors(cond8, w_mod, b_mod, layer):
    d, n6 = w_mod.shape[1], w_mod.shape[2]
    tn = _tile(n6, 512)
    out = pl.pallas_call(
        _mod_kernel, name="mod_vectors",
        grid=(n6 // tn,),
        in_specs=[pl.BlockSpec((8, d), lambda j: (0, 0)),
                  pl.BlockSpec((None, d, tn), lambda j: (layer, 0, j)),
                  pl.BlockSpec((None, 1, tn), lambda j: (layer, 0, j))],
        out_specs=pl.BlockSpec((8, tn), lambda j: (0, j)),
        out_shape=jax.ShapeDtypeStruct((8, n6), F32),
        compiler_params=_cparams("parallel"),
    )(cond8, w_mod, b_mod.reshape(b_mod.shape[0], 1, n6))
    return out.reshape(8, 1, n6)


def _modulate_rows(x, g, sc, sh):
    r = lax.rsqrt(jnp.mean(x * x, axis=-1, keepdims=True) + EPS)
    return x * r * (g * (1.0 + sc)) + sh


def _modnorm_kernel(x_ref, g_ref, sc_ref, sh_ref, o_ref):
    o_ref[...] = _modulate_rows(x_ref[...], g_ref[...], sc_ref[...], sh_ref[...]).astype(o_ref.dtype)


def _modnorm(x, g, mod3, k_shift, k_scale, out_dtype):
    b, n, d = x.shape
    tr = _tile(n, 256, 8)
    return pl.pallas_call(
        _modnorm_kernel, name="modnorm",
        grid=(b, n // tr),
        in_specs=[pl.BlockSpec((None, tr, d), lambda i, t: (i, t, 0)),
                  pl.BlockSpec((1, d), lambda i, t: (0, 0)),
                  pl.BlockSpec((None, 1, d), lambda i, t: (i, 0, k_scale)),
                  pl.BlockSpec((None, 1, d), lambda i, t: (i, 0, k_shift))],
        out_specs=pl.BlockSpec((None, tr, d), lambda i, t: (i, t, 0)),
        out_shape=jax.ShapeDtypeStruct((b, n, d), out_dtype),
        compiler_params=_cparams("parallel", "parallel"),
    )(x, g.reshape(1, d), mod3, mod3).reshape(b * n, d)


def _modnorm_seq_kernel(x_ref, c_ref, g_ref, sc_ref, sh_ref, o_ref, *, n_ctx_tiles):
    is_ctx = pl.program_id(1) < n_ctx_tiles
    x = jnp.where(is_ctx, c_ref[...], x_ref[...])
    o_ref[...] = _modulate_rows(x, g_ref[...], sc_ref[...], sh_ref[...]).astype(o_ref.dtype)


def _modnorm_seq(x, ctx, g, mod3, ctx_row, k_shift, k_scale):
    b, n, d = x.shape
    c = ctx.shape[1]
    tr = _tile(c, 256, 8)
    assert n % tr == 0
    nct = c // tr

    def mod_row(i, t):
        return jnp.where(t < nct, ctx_row, i)

    return pl.pallas_call(
        functools.partial(_modnorm_seq_kernel, n_ctx_tiles=nct), name="modnorm_seq",
        grid=(b, (c + n) // tr),
        in_specs=[pl.BlockSpec((None, tr, d), lambda i, t: (i, jnp.maximum(t - nct, 0), 0)),
                  pl.BlockSpec((None, tr, d), lambda i, t: (i, jnp.minimum(t, nct - 1), 0)),
                  pl.BlockSpec((1, d), lambda i, t: (0, 0)),
                  pl.BlockSpec((None, 1, d), lambda i, t: (mod_row(i, t), 0, k_scale)),
                  pl.BlockSpec((None, 1, d), lambda i, t: (mod_row(i, t), 0, k_shift))],
        out_specs=pl.BlockSpec((None, tr, d), lambda i, t: (i, t, 0)),
        out_shape=jax.ShapeDtypeStruct((b, c + n, d), MXU_DTYPE),
        compiler_params=_cparams("parallel", "parallel"),
    )(x, ctx, g.reshape(1, d), mod3, mod3)


def _mm_kernel(a_ref, w_ref, o_ref):
    o_ref[...] = _dot(a_ref[...], w_ref[...]).astype(o_ref.dtype)


def _matmul(a, w, out_dtype, tm_pref=1024, tn_pref=1024):
    m, k = a.shape
    n = w.shape[1]
    tm, tn = _tile(m, tm_pref, 8), _tile(n, tn_pref)
    return pl.pallas_call(
        _mm_kernel, name="matmul",
        grid=(m // tm, n // tn),
        in_specs=[pl.BlockSpec((tm, k), lambda i, j: (i, 0)),
                  pl.BlockSpec((k, tn), lambda i, j: (0, j))],
        out_specs=pl.BlockSpec((tm, tn), lambda i, j: (i, j)),
        out_shape=jax.ShapeDtypeStruct((m, n), out_dtype),
        compiler_params=_cparams("parallel", "parallel"),
    )(a, w)


def _ffn_up_kernel(a_ref, wg_ref, wu_ref, o_ref):
    a = a_ref[...]
    gate = _dot(a, wg_ref[...])
    up = _dot(a, wu_ref[...])
    o_ref[...] = (gate * jax.nn.sigmoid(gate) * up).astype(o_ref.dtype)


def _ffn_up(a, wg, wu):
    m, k = a.shape
    f = wg.shape[1]
    tm, tn = _tile(m, 1024, 8), _tile(f, 512)
    return pl.pallas_call(
        _ffn_up_kernel, name="ffn_up",
        grid=(m // tm, f // tn),
        in_specs=[pl.BlockSpec((tm, k), lambda i, j: (i, 0)),
                  pl.BlockSpec((k, tn), lambda i, j: (0, j)),
                  pl.BlockSpec((k, tn), lambda i, j: (0, j))],
        out_specs=pl.BlockSpec((tm, tn), lambda i, j: (i, j)),
        out_shape=jax.ShapeDtypeStruct((m, f), MXU_DTYPE),
        compiler_params=_cparams("parallel", "parallel"),
    )(a, wg, wu)


def _mm_res_kernel(a_ref, w_ref, r_ref, g_ref, o_ref, acc_ref):
    kk = pl.program_id(2)

    @pl.when(kk == 0)
    def _():
        acc_ref[...] = jnp.zeros_like(acc_ref)

    acc_ref[...] += _dot(a_ref[...], w_ref[...])

    @pl.when(kk == pl.num_programs(2) - 1)
    def _():
        o_ref[...] = r_ref[...] + g_ref[...] * acc_ref[...]


def _matmul_residual(a, w, res, mod3, k_gate, rows_per_sample, tk_pref):
    m, k = a.shape
    n = w.shape[1]
    tm, tn, tk = _tile(rows_per_sample, 1024, 8), _tile(n, 1024), _tile(k, tk_pref)
    tps = rows_per_sample // tm
    return pl.pallas_call(
        _mm_res_kernel, name="ffn_down_residual",
        grid=(m // tm, n // tn, k // tk),
        in_specs=[pl.BlockSpec((tm, tk), lambda i, j, kk: (i, kk)),
                  pl.BlockSpec((tk, tn), lambda i, j, kk: (kk, j)),
                  pl.BlockSpec((tm, tn), lambda i, j, kk: (i, j)),
                  pl.BlockSpec((None, 1, tn), lambda i, j, kk: (i // tps, 0, k_gate * (n // tn) + j))],
        out_specs=pl.BlockSpec((tm, tn), lambda i, j, kk: (i, j)),
        out_shape=jax.ShapeDtypeStruct((m, n), F32),
        scratch_shapes=[pltpu.VMEM((tm, tn), F32)],
        compiler_params=_cparams("parallel", "parallel", "arbitrary"),
    )(a, w, res, mod3)


def _mm2_res_kernel(a1_ref, a2_ref, w1_ref, w2_ref, r_ref, g_ref, o_ref):
    acc = _dot(a1_ref[...], w1_ref[...]) + _dot(a2_ref[...], w2_ref[...])
    o_ref[...] = r_ref[...] + g_ref[...] * acc


def _out_proj_residual(a1, a2, w, res, mod3, k_gate, rows_per_sample):
    m, kh = a1.shape
    n = w.shape[1]
    assert w.shape[0] == 2 * kh
    tm, tn = _tile(rows_per_sample, 1024, 8), _tile(n, 512)
    tps = rows_per_sample // tm
    return pl.pallas_call(
        _mm2_res_kernel, name="out_proj_residual",
        grid=(m // tm, n // tn),
        in_specs=[pl.BlockSpec((tm, kh), lambda i, j: (i, 0)),
                  pl.BlockSpec((tm, kh), lambda i, j: (i, 0)),
                  pl.BlockSpec((kh, tn), lambda i, j: (0, j)),
                  pl.BlockSpec((kh, tn), lambda i, j: (1, j)),
                  pl.BlockSpec((tm, tn), lambda i, j: (i, j)),
                  pl.BlockSpec((None, 1, tn), lambda i, j: (i // tps, 0, k_gate * (n // tn) + j))],
        out_specs=pl.BlockSpec((tm, tn), lambda i, j: (i, j)),
        out_shape=jax.ShapeDtypeStruct((m, n), F32),
        compiler_params=_cparams("parallel", "parallel"),
    )(a1, a2, w, w, res, mod3)


def _pool_mm_res_kernel(a_ref, w_ref, ps_ref, r_ref, g_ref, o_ref):
    o_ref[...] = r_ref[...] + g_ref[...] * (_dot(a_ref[...], w_ref[...]) * ps_ref[...])


def _pool_proj_residual(pooled, w_pool, pool_scale, res, mod3, k_gate, rows_per_sample):
    ng, m, gw = pooled.shape
    d = ng * gw
    tm, tn = _tile(rows_per_sample, 1024, 8), _tile(gw, 1024)
    tpg = gw // tn
    tps = rows_per_sample // tm
    return pl.pallas_call(
        _pool_mm_res_kernel, name="pool_proj_residual",
        grid=(m // tm, d // tn),
        in_specs=[pl.BlockSpec((None, tm, gw), lambda i, j: (j // tpg, i, 0)),
                  pl.BlockSpec((None, gw, tn), lambda i, j: (j // tpg, 0, j % tpg)),
                  pl.BlockSpec((1, tn), lambda i, j: (0, j)),
                  pl.BlockSpec((tm, tn), lambda i, j: (i, j)),
                  pl.BlockSpec((None, 1, tn), lambda i, j: (i // tps, 0, k_gate * (d // tn) + j))],
        out_specs=pl.BlockSpec((tm, tn), lambda i, j: (i, j)),
        out_shape=jax.ShapeDtypeStruct((m, d), F32),
        compiler_params=_cparams("parallel", "parallel"),
    )(pooled, w_pool, pool_scale.reshape(1, d), res, mod3)


def _pool_kernel(h_ref, o_ref, pad_ref, *, n, rows, tiles_per_group):
    zeros = jnp.zeros((SEQ_PAD, pad_ref.shape[1]), F32)
    pad_ref[0:SEQ_PAD, :] = zeros
    pad_ref[SEQ_PAD + n:2 * SEQ_PAD + n, :] = zeros
    for r0 in range(0, n, rows):
        pad_ref[SEQ_PAD + r0:SEQ_PAD + r0 + rows, :] = h_ref[r0:r0 + rows, :]
    group = pl.program_id(1) // tiles_per_group

    def pooled(win):
        left, right = win // 2, win - 1 - win // 2
        for r0 in range(0, n, rows):
            t = r0 + lax.broadcasted_iota(jnp.int32, (rows, 1), 0)
            acc = pad_ref[SEQ_PAD + r0 - left:SEQ_PAD + r0 - left + rows, :]
            for dlt in range(-left + 1, right + 1):
                acc = acc + pad_ref[SEQ_PAD + r0 + dlt:SEQ_PAD + r0 + dlt + rows, :]
            cnt = jnp.minimum(t + right + 1, n) - jnp.maximum(t - left, 0)
            mean = acc / cnt.astype(F32)
            o_ref[r0:r0 + rows, :] = (mean - h_ref[r0:r0 + rows, :]).astype(o_ref.dtype)

    for gi, win in enumerate(POOL_WINDOWS):
        assert win // 2 <= SEQ_PAD

        @pl.when(group == gi)
        def _(win=win):
            pooled(win)


def _pool(h, b, n):
    d = h.shape[1]
    ng = len(POOL_WINDOWS)
    gw = d // ng
    tc = _tile(gw, 256)
    rows = _tile(n, 1024, 8)
    tpg = gw // tc
    out = pl.pallas_call(
        functools.partial(_pool_kernel, n=n, rows=rows, tiles_per_group=tpg), name="pool",
        grid=(b, d // tc),
        in_specs=[pl.BlockSpec((None, n, tc), lambda i, j: (i, 0, j))],
        out_specs=pl.BlockSpec((None, None, n, tc), lambda i, j: (j // tpg, i, 0, j % tpg)),
        out_shape=jax.ShapeDtypeStruct((ng, b, n, gw), MXU_DTYPE),
        scratch_shapes=[pltpu.VMEM((n + 2 * SEQ_PAD, tc), F32)],
        compiler_params=_cparams("parallel", "parallel"),
    )(h.reshape(b, n, d))
    return out.reshape(ng, b * n, gw)


def _rope_tables(n_ctx, n_lat):
    ax = HEAD_DIM // 2
    inv = jnp.float32(ROPE_THETA) ** (-jnp.arange(0, ax, 2, dtype=F32) / ax)
    t = jnp.arange(n_lat)
    ang = jnp.concatenate([(t // GRID_W).astype(F32)[:, None] * inv,
                           (t % GRID_W).astype(F32)[:, None] * inv], axis=-1)
    cos = jnp.repeat(jnp.cos(ang), 2, axis=-1)
    sin = jnp.repeat(jnp.sin(ang), 2, axis=-1) * jnp.tile(jnp.array([-1.0, 1.0], F32), HEAD_DIM // 2)
    cos = jnp.concatenate([jnp.ones((n_ctx, HEAD_DIM), F32), cos], axis=0)
    sin = jnp.concatenate([jnp.zeros((n_ctx, HEAD_DIM), F32), sin], axis=0)
    return cos, sin


def _qk_prep_kernel(q_ref, k_ref, v_ref, cos_ref, sin_ref, qn_ref, kn_ref, qo_ref, ko_ref, vo_ref):
    cos, sin = cos_ref[...], sin_ref[...]
    even = (lax.broadcasted_iota(jnp.int32, cos.shape, 1) & 1) == 0

    def norm_rope(x, w):
        x = x * lax.rsqrt(jnp.mean(x * x, axis=-1, keepdims=True) + EPS) * w
        partner = jnp.where(even, pltpu.roll(x, HEAD_DIM - 1, 1), pltpu.roll(x, 1, 1))
        return x * cos + partner * sin

    scale = HEAD_DIM ** -0.5
    for h in range(q_ref.shape[1] // HEAD_DIM):
        sl = slice(h * HEAD_DIM, (h + 1) * HEAD_DIM)
        qo_ref[:, sl] = (norm_rope(q_ref[:, sl], qn_ref[...]) * scale).astype(qo_ref.dtype)
    for h in range(k_ref.shape[1] // HEAD_DIM):
        sl = slice(h * HEAD_DIM, (h + 1) * HEAD_DIM)
        ko_ref[:, sl] = norm_rope(k_ref[:, sl], kn_ref[...]).astype(ko_ref.dtype)
    vo_ref[...] = v_ref[...].astype(vo_ref.dtype)


def _qk_prep(p, wq, wk, cos, sin, q_norm, k_norm):
    b, s, _ = p.shape
    tr = _tile(s, 512, 8)
    return pl.pallas_call(
        _qk_prep_kernel, name="qk_prep",
        grid=(b, s // tr),
        in_specs=[pl.BlockSpec((None, tr, wq), lambda i, t: (i, t, 0)),
                  pl.BlockSpec((None, tr, wk), lambda i, t: (i, t, wq // wk)),
                  pl.BlockSpec((None, tr, wk), lambda i, t: (i, t, wq // wk + 1)),
                  pl.BlockSpec((tr, HEAD_DIM), lambda i, t: (t, 0)),
                  pl.BlockSpec((tr, HEAD_DIM), lambda i, t: (t, 0)),
                  pl.BlockSpec((1, HEAD_DIM), lambda i, t: (0, 0)),
                  pl.BlockSpec((1, HEAD_DIM), lambda i, t: (0, 0))],
        out_specs=[pl.BlockSpec((None, tr, wq), lambda i, t: (i, t, 0)),
                   pl.BlockSpec((None, tr, wk), lambda i, t: (i, t, 0)),
                   pl.BlockSpec((None, tr, wk), lambda i, t: (i, t, 0))],
        out_shape=[jax.ShapeDtypeStruct((b, s, wq), MXU_DTYPE),
                   jax.ShapeDtypeStruct((b, s, wk), MXU_DTYPE),
                   jax.ShapeDtypeStruct((b, s, wk), MXU_DTYPE)],
        compiler_params=_cparams("parallel", "parallel"),
    )(p, p, p, cos, sin, q_norm.reshape(1, HEAD_DIM), k_norm.reshape(1, HEAD_DIM))


def _attn_kernel(q_ref, k_ref, v_ref, o_ref, m_ref, l_ref, acc_ref):
    ki = pl.program_id(3)
    tq = q_ref.shape[0]

    @pl.when(ki == 0)
    def _():
        m_ref[...] = jnp.full_like(m_ref, -jnp.inf)
        l_ref[...] = jnp.zeros_like(l_ref)
        acc_ref[...] = jnp.zeros_like(acc_ref)

    q = jnp.concatenate([q_ref[:, g * HEAD_DIM:(g + 1) * HEAD_DIM] for g in range(GQA_GROUP)], axis=0)
    s = _dot_nt(q, k_ref[...])
    m_prev = m_ref[...]
    m_new = jnp.maximum(m_prev, jnp.max(s, axis=-1, keepdims=True))
    alpha = jnp.exp(m_prev - m_new)
    p = jnp.exp(s - m_new)
    l_ref[...] = alpha * l_ref[...] + jnp.sum(p, axis=-1, keepdims=True)
    acc_ref[...] = alpha * acc_ref[...] + _dot(p.astype(v_ref.dtype), v_ref[...])
    m_ref[...] = m_new

    @pl.when(ki == pl.num_programs(3) - 1)
    def _():
        out = acc_ref[...] / l_ref[...]
        for g in range(GQA_GROUP):
            o_ref[:, g * HEAD_DIM:(g + 1) * HEAD_DIM] = out[g * tq:(g + 1) * tq].astype(o_ref.dtype)


def _attention(q, k, v, n_ctx):
    b, s, wq = q.shape
    n = s - n_ctx
    kvh = k.shape[2] // HEAD_DIM
    gw = GQA_GROUP * HEAD_DIM
    tq = _tile(n_ctx, 256, 8)
    assert n % tq == 0
    tk = _tile(s, 768)
    qoff = n_ctx // tq
    return pl.pallas_call(
        _attn_kernel, name="attention",
        grid=(b, kvh, n // tq, s // tk),
        in_specs=[pl.BlockSpec((None, tq, gw), lambda i, h, qi, ki: (i, qi + qoff, h)),
                  pl.BlockSpec((None, tk, HEAD_DIM), lambda i, h, qi, ki: (i, ki, h)),
                  pl.BlockSpec((None, tk, HEAD_DIM), lambda i, h, qi, ki: (i, ki, h))],
        out_specs=pl.BlockSpec((None, tq, gw), lambda i, h, qi, ki: (i, qi, h)),
        out_shape=jax.ShapeDtypeStruct((b, n, wq), MXU_DTYPE),
        scratch_shapes=[pltpu.VMEM((GQA_GROUP * tq, 1), F32),
                        pltpu.VMEM((GQA_GROUP * tq, 1), F32),
                        pltpu.VMEM((GQA_GROUP * tq, HEAD_DIM), F32)],
        compiler_params=_cparams("parallel", "parallel", "parallel", "arbitrary"),
    )(q, k, v)


def _dn_prep_kernel(x_ref, w_ref, o_ref, pad_ref, *, s, n_ctx, rows, nh):
    j = pl.program_id(1)
    zeros = jnp.zeros((SEQ_PAD, HEAD_DIM), F32)
    pad_ref[0:SEQ_PAD, :] = zeros
    pad_ref[SEQ_PAD + s:2 * SEQ_PAD + s, :] = zeros
    for r0 in range(0, s, rows):
        pad_ref[SEQ_PAD + r0:SEQ_PAD + r0 + rows, :] = x_ref[r0:r0 + rows, :]
    w = w_ref[...]
    qk_scale = jnp.where(j < nh, HEAD_DIM ** -0.5, 1.0).astype(F32)
    for r0 in range(0, s, rows):
        t = r0 + lax.broadcasted_iota(jnp.int32, (rows, 1), 0)
        seg_lo = jnp.where(t < n_ctx, 0, n_ctx)
        seg_hi = jnp.where(t < n_ctx, n_ctx, s)
        acc = jnp.zeros((rows, HEAD_DIM), F32)
        for tap, dlt in enumerate(CONV_TAPS):
            xs = pad_ref[SEQ_PAD + r0 + dlt:SEQ_PAD + r0 + dlt + rows, :]
            valid = (t + dlt >= seg_lo) & (t + dlt < seg_hi)
            acc = acc + jnp.where(valid, xs, 0.0) * w[tap:tap + 1, :]
        y = acc * jax.nn.sigmoid(acc)
        nrm = lax.rsqrt(jnp.sum(y * y, axis=-1, keepdims=True) + EPS) * qk_scale
        o_ref[r0:r0 + rows, :] = y * jnp.where(j < 2 * nh, nrm, 1.0)


def _dn_prep(p, conv_w, col0, n_ctx, nh):
    b, s, _ = p.shape
    rows = _tile(s, 1536, 8)
    base = col0 // HEAD_DIM
    return pl.pallas_call(
        functools.partial(_dn_prep_kernel, s=s, n_ctx=n_ctx, rows=rows, nh=nh), name="dn_prep",
        grid=(b, 3 * nh),
        in_specs=[pl.BlockSpec((None, s, HEAD_DIM), lambda i, j: (i, 0, base + j)),
                  pl.BlockSpec((len(CONV_TAPS), HEAD_DIM), lambda i, j: (0, j))],
        out_specs=pl.BlockSpec((None, None, s, HEAD_DIM), lambda i, j: (i, j, 0, 0)),
        out_shape=jax.ShapeDtypeStruct((b, 3 * nh, s, HEAD_DIM), F32),
        scratch_shapes=[pltpu.VMEM((s + 2 * SEQ_PAD, HEAD_DIM), F32)],
        compiler_params=_cparams("parallel", "parallel"),
    )(p, conv_w)


def _chunk_masks(g, chunk, reverse):
    ri = lax.broadcasted_iota(jnp.int32, (g, g), 0)
    ci = lax.broadcasted_iota(jnp.int32, (g, g), 1)
    sh = chunk.bit_length() - 1
    same = lax.shift_right_logical(ri, sh) == lax.shift_right_logical(ci, sh)
    if reverse:
        return same & (ci >= ri), same & (ci > ri), ri, ci
    return same & (ci <= ri), same & (ci < ri), ri, ci


def _gate_kernel(raw_ref, alog_ref, dtb_ref, col_ref, row_ref, *, nh):
    raw = raw_ref[...]
    g_rows = raw.shape[0]
    lane = lax.broadcasted_iota(jnp.int32, raw.shape, 1)
    xs = raw + dtb_ref[...]
    softplus = jnp.maximum(xs, 0.0) + jnp.log1p(jnp.exp(-jnp.abs(xs)))
    gl = jnp.where(lane < 2 * nh, -jnp.exp(alog_ref[...]) * softplus, 0.0)
    beta = jax.nn.sigmoid(raw)
    incl_f, _, _, _ = _chunk_masks(g_rows, DN_CHUNK, False)
    incl_b, _, _, _ = _chunk_masks(g_rows, DN_CHUNK, True)
    cum_f = _dot_exact_lhs(incl_f.astype(MXU_DTYPE), gl)
    cum_b = _dot_exact_lhs(incl_b.astype(MXU_DTYPE), gl)
    total = cum_f + cum_b - gl
    col = jnp.where(lane < nh, cum_f,
                    jnp.where(lane < 2 * nh, cum_b,
                              jnp.where(lane < 4 * nh, beta, pltpu.roll(total, 4 * nh, 1))))
    col_ref[...] = col
    row_ref[...] = col.T


def _dn_gates(raw, a_log, dt_bias, nh):
    b, s, lanes = raw.shape
    assert 6 * nh <= lanes and s % DN_GROUP == 0
    pad = lambda v: jnp.pad(v.reshape(1, 2 * nh).astype(F32), ((0, 0), (0, lanes - 2 * nh)))
    return pl.pallas_call(
        functools.partial(_gate_kernel, nh=nh), name="dn_gates",
        grid=(b, s // DN_GROUP),
        in_specs=[pl.BlockSpec((None, DN_GROUP, lanes), lambda i, t: (i, t, 0)),
                  pl.BlockSpec((1, lanes), lambda i, t: (0, 0)),
                  pl.BlockSpec((1, lanes), lambda i, t: (0, 0))],
        out_specs=[pl.BlockSpec((None, DN_GROUP, lanes), lambda i, t: (i, t, 0)),
                   pl.BlockSpec((None, lanes, DN_GROUP), lambda i, t: (i, 0, t))],
        out_shape=[jax.ShapeDtypeStruct((b, s, lanes), F32),
                   jax.ShapeDtypeStruct((b, lanes, s), F32)],
        compiler_params=_cparams("parallel", "parallel"),
    )(raw, pad(a_log), pad(dt_bias))


def _unit_lower_inverse(lmat, ri, ci):
    eye = (ri == ci).astype(F32)
    sh = DN_DIAG.bit_length() - 1
    diag = lax.shift_right_logical(ri, sh) == lax.shift_right_logical(ci, sh)
    dm = jnp.where(diag, lmat, 0.0)
    off = lmat - dm
    inv_d = eye - dm
    power = dm
    span = 2
    while span <= DN_DIAG:
        power = _dot3(power, power)
        inv_d = inv_d + _dot3(inv_d, power)
        span *= 2
    mm = _dot3(inv_d, off)
    inv_m = eye - mm
    power = mm
    span = 2
    while span < DN_CHUNK // DN_DIAG:
        power = _dot3(power, power)
        inv_m = inv_m + _dot3(inv_m, power)
        span *= 2
    return _dot3(inv_m, inv_d)


def _dn_group(direction, head, nh, q, k, v, gcol, grow_ref, state_ref):
    g_rows = q.shape[0]
    n_chunks = g_rows // DN_CHUNK
    lane = lax.broadcasted_iota(jnp.int32, gcol.shape, 1)
    idx = direction * nh + head

    def column(i):
        return jnp.sum(jnp.where(lane == i, gcol, 0.0), axis=1, keepdims=True)

    gc, beta, total = column(idx), column(2 * nh + idx), column(4 * nh + idx)
    gr = grow_ref[pl.ds(idx, 1), :]
    incl, strict, ri, ci = _chunk_masks(g_rows, DN_CHUNK, direction == 1)
    decay = jnp.exp(jnp.where(incl, gc - gr, -jnp.inf))
    kbeta = k * beta
    k16 = k.astype(MXU_DTYPE)
    lmat = jnp.where(strict, _dot_nt(kbeta.astype(MXU_DTYPE), k16) * decay, 0.0)
    qk = _dot_nt(q.astype(MXU_DTYPE), k16) * decay
    tinv = _unit_lower_inverse(lmat, ri, ci)
    eg = jnp.exp(gc)
    sol = _dot3(tinv, jnp.concatenate([v * beta, kbeta * eg], axis=1))
    u, w16 = sol[:, :HEAD_DIM], sol[:, HEAD_DIM:].astype(MXU_DTYPE)
    qd16 = (q * eg).astype(MXU_DTYPE)
    kdt16 = (k * jnp.exp(total - gc)).T.astype(MXU_DTYPE)

    state = state_ref[...]
    v_new, inter = [None] * n_chunks, [None] * n_chunks
    order = range(n_chunks - 1, -1, -1) if direction == 1 else range(n_chunks)
    for c in order:
        lo, hi = c * DN_CHUNK, (c + 1) * DN_CHUNK
        s16 = state.astype(MXU_DTYPE)
        vn = u[lo:hi] - _dot(w16[lo:hi], s16)
        inter[c] = _dot(qd16[lo:hi], s16)
        v_new[c] = vn
        pieces = []
        if lo:
            pieces.append(jnp.zeros((lo, HEAD_DIM), F32))
        pieces.append(vn)
        if g_rows - hi:
            pieces.append(jnp.zeros((g_rows - hi, HEAD_DIM), F32))
        placed = jnp.concatenate(pieces, axis=0).astype(MXU_DTYPE)
        state = state * jnp.exp(total[lo:lo + 1, :]) + _dot(kdt16, placed)
    state_ref[...] = state
    v_all = jnp.concatenate(v_new, axis=0).astype(MXU_DTYPE)
    return jnp.concatenate(inter, axis=0) + _dot(qk.astype(MXU_DTYPE), v_all)


def _dn_kernel(qf, kf, vf, gcf, grf, qb, kb, vb, gcb, grb, of_ref, ob_ref, state_ref, *, nh):
    head = pl.program_id(0) % nh

    @pl.when(pl.program_id(1) == 0)
    def _():
        state_ref[...] = jnp.zeros_like(state_ref)

    of_ref[...] = _dn_group(0, head, nh, qf[...], kf[...], vf[...], gcf[...], grf, state_ref.at[0])
    ob_ref[...] = _dn_group(1, head, nh, qb[...], kb[...], vb[...], gcb[...], grb, state_ref.at[1])


def _deltanet(qkv, gcol, grow, n_ctx, nh):
    b, _, s, _ = qkv.shape
    lanes = gcol.shape[2]
    assert n_ctx % DN_GROUP == 0 and s % DN_GROUP == 0
    nc, nt = n_ctx // DN_GROUP, s // DN_GROUP
    n = s - n_ctx

    def fwd(t):
        return t

    def bwd(t):
        return jnp.where(t < nc, nc - 1 - t, nc + nt - 1 - t)

    def specs(pos):
        return [pl.BlockSpec((None, None, DN_GROUP, HEAD_DIM), lambda bh, t: (bh // nh, bh % nh, pos(t), 0)),
                pl.BlockSpec((None, None, DN_GROUP, HEAD_DIM), lambda bh, t: (bh // nh, nh + bh % nh, pos(t), 0)),
                pl.BlockSpec((None, None, DN_GROUP, HEAD_DIM), lambda bh, t: (bh // nh, 2 * nh + bh % nh, pos(t), 0)),
                pl.BlockSpec((None, DN_GROUP, lanes), lambda bh, t: (bh // nh, pos(t), 0)),
                pl.BlockSpec((None, lanes, DN_GROUP), lambda bh, t: (bh // nh, 0, pos(t)))]

    def out_spec(first, pos):
        return pl.BlockSpec((None, None, DN_GROUP, HEAD_DIM),
                            lambda bh, t: (bh // nh, bh % nh, jnp.where(t < nc, first, pos(t) - nc), 0))

    out = jax.ShapeDtypeStruct((b, nh, n, HEAD_DIM), F32)
    return pl.pallas_call(
        functools.partial(_dn_kernel, nh=nh), name="deltanet",
        grid=(b * nh, nt),
        in_specs=specs(fwd) + specs(bwd),
        out_specs=[out_spec(0, fwd), out_spec(nt - 1 - nc, bwd)],
        out_shape=[out, out],
        scratch_shapes=[pltpu.VMEM((2, HEAD_DIM, HEAD_DIM), F32)],
        compiler_params=_cparams("parallel", "arbitrary"),
    )(qkv, qkv, qkv, gcol, grow, qkv, qkv, qkv, gcol, grow)


def _gated_norm_kernel(of_ref, ob_ref, z_ref, w_ref, o_ref):
    o = of_ref[...] + ob_ref[...]
    z = z_ref[...]
    y = o * lax.rsqrt(jnp.mean(o * o, axis=-1, keepdims=True) + EPS) * w_ref[...] * (z * jax.nn.sigmoid(z))
    o_ref[...] = y.astype(o_ref.dtype)


def _gated_norm(o_f, o_b, p, z_col0, n_ctx, o_norm):
    b, nh, n, _ = o_f.shape
    tr = _tile(n_ctx, 2048, 8)
    assert n % tr == 0
    roff, cbase = n_ctx // tr, z_col0 // HEAD_DIM
    return pl.pallas_call(
        _gated_norm_kernel, name="gated_norm",
        grid=(b, nh, n // tr),
        in_specs=[pl.BlockSpec((None, None, tr, HEAD_DIM), lambda i, h, r: (i, h, r, 0)),
                  pl.BlockSpec((None, None, tr, HEAD_DIM), lambda i, h, r: (i, h, r, 0)),
                  pl.BlockSpec((None, tr, HEAD_DIM), lambda i, h, r: (i, r + roff, cbase + h)),
                  pl.BlockSpec((1, HEAD_DIM), lambda i, h, r: (0, 0))],
        out_specs=pl.BlockSpec((None, tr, HEAD_DIM), lambda i, h, r: (i, r, h)),
        out_shape=jax.ShapeDtypeStruct((b, n, nh * HEAD_DIM), MXU_DTYPE),
        compiler_params=_cparams("parallel", "parallel", "parallel"),
    )(o_f, o_b, p, o_norm.reshape(1, HEAD_DIM))


def _ffn(x, b, n, norm_g, mod3, w_gate, w_up, w_down):
    d = x.shape[1]
    f = w_gate.shape[1]
    fp = -(-f // FFN_ALIGN) * FFN_ALIGN
    wg = jnp.pad(w_gate.astype(MXU_DTYPE), ((0, 0), (0, fp - f)))
    wu = jnp.pad(w_up.astype(MXU_DTYPE), ((0, 0), (0, fp - f)))
    wd = jnp.pad(w_down.astype(MXU_DTYPE), ((0, fp - f), (0, 0)))
    h = _modnorm(x.reshape(b, n, d), norm_g, mod3, 3, 4, MXU_DTYPE)
    hid = _ffn_up(h, wg, wu)
    return _matmul_residual(hid, wd, x, mod3, 5, n, 2816)


def _hybrid_layer(x, ctx, c, c_ctx, w_mod, b_mod, norm_mix, norm_ffn, w_in, w_out, q_norm, k_norm,
                  conv_w, a_log, dt_bias, o_norm, w_gate, w_up, w_down, layer):
    b, n, d = x.shape
    n_ctx = ctx.shape[1]
    s = n_ctx + n
    wq = d // 2
    wk = wq // GQA_GROUP
    nh = wq // HEAD_DIM
    main = (wq + 2 * wk) + 4 * wq
    assert w_in.shape[1] == main + 4 * nh

    cond8 = jnp.zeros((8, d), F32).at[:b].set(c).at[b].set(c_ctx)
    mod3 = _mod_vectors(cond8, w_mod, b_mod, layer)
    h_all = _modnorm_seq(x, ctx, norm_mix, mod3, b, 0, 1).reshape(b * s, d)

    w_main = w_in[:, :main].astype(MXU_DTYPE)
    w_gates = jnp.pad(w_in[:, main:].astype(MXU_DTYPE), ((0, 0), (0, HEAD_DIM - 4 * nh)))
    p = _matmul(h_all, w_main, F32, 768, 1024).reshape(b, s, main)
    raw_gates = _matmul(h_all, w_gates, F32, 768, HEAD_DIM).reshape(b, s, HEAD_DIM)

    cos, sin = _rope_tables(n_ctx, n)
    qa, ka, va = _qk_prep(p, wq, wk, cos, sin, q_norm, k_norm)
    ya = _attention(qa, ka, va, n_ctx)

    qkv = _dn_prep(p, conv_w, wq + 2 * wk, n_ctx, nh)
    gcol, grow = _dn_gates(raw_gates, a_log, dt_bias, nh)
    o_f, o_b = _deltanet(qkv, gcol, grow, n_ctx, nh)
    yb = _gated_norm(o_f, o_b, p, wq + 2 * wk + 3 * wq, n_ctx, o_norm)

    x1 = _out_proj_residual(ya.reshape(b * n, wq), yb.reshape(b * n, wq), w_out.astype(MXU_DTYPE),
                            x.reshape(b * n, d), mod3, 2, n)
    return _ffn(x1, b, n, norm_ffn, mod3, w_gate, w_up, w_down)


def _pool_layer(x, b, n, c, w_mod, b_mod, norm_mix, norm_ffn, w_pool, pool_scale, w_gate, w_up, w_down, layer):
    d = x.shape[1]
    cond8 = jnp.zeros((8, d), F32).at[:b].set(c)
    mod3 = _mod_vectors(cond8, w_mod, b_mod, layer)
    h = _modnorm(x.reshape(b, n, d), norm_mix, mod3, 0, 1, F32)
    pooled = _pool(h, b, n)
    x1 = _pool_proj_residual(pooled, w_pool.astype(MXU_DTYPE), pool_scale, x, mod3, 2, n)
    return _ffn(x1, b, n, norm_ffn, mod3, w_gate, w_up, w_down)


def kernel(x, c, ctx, c_ctx, w_mod, b_mod, norm_mix, norm_ffn, w_in, w_out, q_norm, k_norm, conv_w, a_log,
           dt_bias, o_norm, w_pool, pool_scale, w_gate, w_up, w_down):
    b, n, d = x.shape
    depth = w_mod.shape[0]
    assert depth == 2
    xf = _hybrid_layer(x, ctx, c, c_ctx, w_mod, b_mod, norm_mix[0], norm_ffn[0], w_in[0], w_out[0], q_norm[0],
                       k_norm[0], conv_w[0], a_log[0], dt_bias[0], o_norm[0], w_gate[0], w_up[0], w_down[0], 0)
    xf = _pool_layer(xf, b, n, c, w_mod, b_mod, norm_mix[1], norm_ffn[1], w_pool[0], pool_scale[0],
                     w_gate[1], w_up[1], w_down[1], 1)
    return xf.reshape(b, n, d)
```

```python
import functools

import jax
import jax.numpy as jnp
from jax import lax
from jax.experimental import pallas as pl
from jax.experimental.pallas import tpu as pltpu

F32 = jnp.float32
MXU_DTYPE = jnp.bfloat16

EPS = 1e-6
HEAD_DIM = 128
GQA_GROUP = 4
GRID_W = 64
ROPE_THETA = 10000.0
LOG2_E = 1.4426950408889634
CONV_TAPS = (-2, -1, 0, 1)
DN_CHUNK = 64
DN_GROUP = 256
DN_DIAG = 16
DN_PREP_HEADS = 4
DN_SCAN_HEADS = 8
POOL_WINDOWS = (2, 4, 8, 16)
SEQ_PAD = 8
FFN_ALIGN = 1024
V7X_VMEM_LIMIT = 56 * 1024 * 1024


def _tile(dim, pref, mult=128):
    if dim <= pref:
        return dim
    t = (pref // mult) * mult
    while t >= mult:
        if dim % t == 0:
            return t
        t -= mult
    raise ValueError(f"no tile for {dim} (pref {pref}, mult {mult})")


def _cparams(*sem):
    return pltpu.CompilerParams(dimension_semantics=sem, vmem_limit_bytes=V7X_VMEM_LIMIT)


def _dot(a, b):
    return jnp.dot(a, b, preferred_element_type=F32)


def _bdot(a, b):
    return jnp.einsum("cik,ckj->cij", a, b, preferred_element_type=F32)


def _bdot_nt(a, b):
    return jnp.einsum("cid,cjd->cij", a, b, preferred_element_type=F32)


def _dot_exact_lhs(m, x):
    x1 = x.astype(MXU_DTYPE)
    r1 = x - x1.astype(F32)
    x2 = r1.astype(MXU_DTYPE)
    x3 = (r1 - x2.astype(F32)).astype(MXU_DTYPE)
    return _dot(m, x1) + _dot(m, x2) + _dot(m, x3)


def _mod_kernel(s_ref, w_ref, b_ref, o_ref):
    s = s_ref[...]
    s = s * jax.nn.sigmoid(s)
    o_ref[...] = _dot(s.astype(MXU_DTYPE), w_ref[...].astype(MXU_DTYPE)) + b_ref[...]


def _mod_vectors(cond8, w_mod, b_mod, layer):
    d, n6 = w_mod.shape[1], w_mod.shape[2]
    tn = _tile(n6, 512)
    out = pl.pallas_call(
        _mod_kernel, name="mod_vectors",
        grid=(n6 // tn,),
        in_specs=[pl.BlockSpec((8, d), lambda j: (0, 0)),
                  pl.BlockSpec((None, d, tn), lambda j: (layer, 0, j)),
                  pl.BlockSpec((None, 1, tn), lambda j: (layer, 0, j))],
        out_specs=pl.BlockSpec((8, tn), lambda j: (0, j)),
        out_shape=jax.ShapeDtypeStruct((8, n6), F32),
        compiler_params=_cparams("parallel"),
    )(cond8, w_mod, b_mod.reshape(b_mod.shape[0], 1, n6))
    return out.reshape(8, 1, n6)


def _modulate_rows(x, g, sc, sh):
    r = lax.rsqrt(jnp.mean(x * x, axis=-1, keepdims=True) + EPS)
    return x * r * (g * (1.0 + sc)) + sh


def _modnorm_kernel(x_ref, g_ref, sc_ref, sh_ref, o_ref):
    o_ref[...] = _modulate_rows(x_ref[...], g_ref[...], sc_ref[...], sh_ref[...]).astype(o_ref.dtype)


def _modnorm(x, g, mod3, k_shift, k_scale, out_dtype):
    b, n, d = x.shape
    tr = _tile(n, 256, 8)
    return pl.pallas_call(
        _modnorm_kernel, name="modnorm",
        grid=(b, n // tr),
        in_specs=[pl.BlockSpec((None, tr, d), lambda i, t: (i, t, 0)),
                  pl.BlockSpec((1, d), lambda i, t: (0, 0)),
                  pl.BlockSpec((None, 1, d), lambda i, t: (i, 0, k_scale)),
                  pl.BlockSpec((None, 1, d), lambda i, t: (i, 0, k_shift))],
        out_specs=pl.BlockSpec((None, tr, d), lambda i, t: (i, t, 0)),
        out_shape=jax.ShapeDtypeStruct((b, n, d), out_dtype),
        compiler_params=_cparams("parallel", "parallel"),
    )(x, g.reshape(1, d), mod3, mod3).reshape(b * n, d)


def _modnorm_seq_kernel(x_ref, c_ref, g_ref, sc_ref, sh_ref, o_ref, *, n_ctx_tiles):
    is_ctx = pl.program_id(1) < n_ctx_tiles
    x = jnp.where(is_ctx, c_ref[...], x_ref[...])
    o_ref[...] = _modulate_rows(x, g_ref[...], sc_ref[...], sh_ref[...]).astype(o_ref.dtype)


def _modnorm_seq(x, ctx, g, mod3, ctx_row, k_shift, k_scale):
    b, n, d = x.shape
    c = ctx.shape[1]
    tr = _tile(c, 256, 8)
    assert n % tr == 0
    nct = c // tr

    def mod_row(i, t):
        return jnp.where(t < nct, ctx_row, i)

    return pl.pallas_call(
        functools.partial(_modnorm_seq_kernel, n_ctx_tiles=nct), name="modnorm_seq",
        grid=(b, (c + n) // tr),
        in_specs=[pl.BlockSpec((None, tr, d), lambda i, t: (i, jnp.maximum(t - nct, 0), 0)),
                  pl.BlockSpec((None, tr, d), lambda i, t: (i, jnp.minimum(t, nct - 1), 0)),
                  pl.BlockSpec((1, d), lambda i, t: (0, 0)),
                  pl.BlockSpec((None, 1, d), lambda i, t: (mod_row(i, t), 0, k_scale)),
                  pl.BlockSpec((None, 1, d), lambda i, t: (mod_row(i, t), 0, k_shift))],
        out_specs=pl.BlockSpec((None, tr, d), lambda i, t: (i, t, 0)),
        out_shape=jax.ShapeDtypeStruct((b, c + n, d), MXU_DTYPE),
        compiler_params=_cparams("parallel", "parallel"),
    )(x, ctx, g.reshape(1, d), mod3, mod3)


def _mm_kernel(a_ref, w_ref, o_ref):
    o_ref[...] = _dot(a_ref[...], w_ref[...]).astype(o_ref.dtype)


def _matmul(a, w, out_dtype, tm_pref=1024, tn_pref=1024):
    m, k = a.shape
    n = w.shape[1]
    tm, tn = _tile(m, tm_pref, 8), _tile(n, tn_pref)
    return pl.pallas_call(
        _mm_kernel, name="matmul",
        grid=(m // tm, n // tn),
        in_specs=[pl.BlockSpec((tm, k), lambda i, j: (i, 0)),
                  pl.BlockSpec((k, tn), lambda i, j: (0, j))],
        out_specs=pl.BlockSpec((tm, tn), lambda i, j: (i, j)),
        out_shape=jax.ShapeDtypeStruct((m, n), out_dtype),
        compiler_params=_cparams("parallel", "parallel"),
    )(a, w)


def _ffn_up_kernel(a_ref, wg_ref, wu_ref, o_ref):
    a = a_ref[...]
    gate = _dot(a, wg_ref[...])
    up = _dot(a, wu_ref[...])
    o_ref[...] = (gate * jax.nn.sigmoid(gate) * up).astype(o_ref.dtype)


def _ffn_up(a, wg, wu):
    m, k = a.shape
    f = wg.shape[1]
    tm, tn = _tile(m, 1024, 8), _tile(f, 512)
    return pl.pallas_call(
        _ffn_up_kernel, name="ffn_up",
        grid=(m // tm, f // tn),
        in_specs=[pl.BlockSpec((tm, k), lambda i, j: (i, 0)),
                  pl.BlockSpec((k, tn), lambda i, j: (0, j)),
                  pl.BlockSpec((k, tn), lambda i, j: (0, j))],
        out_specs=pl.BlockSpec((tm, tn), lambda i, j: (i, j)),
        out_shape=jax.ShapeDtypeStruct((m, f), MXU_DTYPE),
        compiler_params=_cparams("parallel", "parallel"),
    )(a, wg, wu)


def _mm_res_kernel(a_ref, w_ref, r_ref, g_ref, o_ref, acc_ref):
    kk = pl.program_id(2)

    @pl.when(kk == 0)
    def _():
        acc_ref[...] = jnp.zeros_like(acc_ref)

    acc_ref[...] += _dot(a_ref[...], w_ref[...])

    @pl.when(kk == pl.num_programs(2) - 1)
    def _():
        o_ref[...] = r_ref[...] + g_ref[...] * acc_ref[...]


def _matmul_residual(a, w, res, mod3, k_gate, rows_per_sample, tk_pref):
    m, k = a.shape
    n = w.shape[1]
    tm, tn, tk = _tile(rows_per_sample, 1024, 8), _tile(n, 1024), _tile(k, tk_pref)
    tps = rows_per_sample // tm
    return pl.pallas_call(
        _mm_res_kernel, name="ffn_down_residual",
        grid=(m // tm, n // tn, k // tk),
        in_specs=[pl.BlockSpec((tm, tk), lambda i, j, kk: (i, kk)),
                  pl.BlockSpec((tk, tn), lambda i, j, kk: (kk, j)),
                  pl.BlockSpec((tm, tn), lambda i, j, kk: (i, j)),
                  pl.BlockSpec((None, 1, tn), lambda i, j, kk: (i // tps, 0, k_gate * (n // tn) + j))],
        out_specs=pl.BlockSpec((tm, tn), lambda i, j, kk: (i, j)),
        out_shape=jax.ShapeDtypeStruct((m, n), F32),
        scratch_shapes=[pltpu.VMEM((tm, tn), F32)],
        compiler_params=_cparams("parallel", "parallel", "arbitrary"),
    )(a, w, res, mod3)


def _mm2_res_kernel(a1_ref, a2_ref, w1_ref, w2_ref, r_ref, g_ref, o_ref):
    acc = _dot(a1_ref[...], w1_ref[...]) + _dot(a2_ref[...], w2_ref[...])
    o_ref[...] = r_ref[...] + g_ref[...] * acc


def _out_proj_residual(a1, a2, w, res, mod3, k_gate, rows_per_sample):
    m, kh = a1.shape
    n = w.shape[1]
    assert w.shape[0] == 2 * kh
    tm, tn = _tile(rows_per_sample, 1024, 8), _tile(n, 512)
    tps = rows_per_sample // tm
    return pl.pallas_call(
        _mm2_res_kernel, name="out_proj_residual",
        grid=(m // tm, n // tn),
        in_specs=[pl.BlockSpec((tm, kh), lambda i, j: (i, 0)),
                  pl.BlockSpec((tm, kh), lambda i, j: (i, 0)),
                  pl.BlockSpec((kh, tn), lambda i, j: (0, j)),
                  pl.BlockSpec((kh, tn), lambda i, j: (1, j)),
                  pl.BlockSpec((tm, tn), lambda i, j: (i, j)),
                  pl.BlockSpec((None, 1, tn), lambda i, j: (i // tps, 0, k_gate * (n // tn) + j))],
        out_specs=pl.BlockSpec((tm, tn), lambda i, j: (i, j)),
        out_shape=jax.ShapeDtypeStruct((m, n), F32),
        compiler_params=_cparams("parallel", "parallel"),
    )(a1, a2, w, w, res, mod3)


def _pool_mm_res_kernel(a_ref, w_ref, ps_ref, r_ref, g_ref, o_ref):
    o_ref[...] = r_ref[...] + g_ref[...] * (_dot(a_ref[...], w_ref[...]) * ps_ref[...])


def _pool_proj_residual(pooled, w_pool, pool_scale, res, mod3, k_gate, rows_per_sample):
    ng, m, gw = pooled.shape
    d = ng * gw
    tm, tn = _tile(rows_per_sample, 1024, 8), _tile(gw, 1024)
    tpg = gw // tn
    tps = rows_per_sample // tm
    return pl.pallas_call(
        _pool_mm_res_kernel, name="pool_proj_residual",
        grid=(m // tm, d // tn),
        in_specs=[pl.BlockSpec((None, tm, gw), lambda i, j: (j // tpg, i, 0)),
                  pl.BlockSpec((None, gw, tn), lambda i, j: (j // tpg, 0, j % tpg)),
                  pl.BlockSpec((1, tn), lambda i, j: (0, j)),
                  pl.BlockSpec((tm, tn), lambda i, j: (i, j)),
                  pl.BlockSpec((None, 1, tn), lambda i, j: (i // tps, 0, k_gate * (d // tn) + j))],
        out_specs=pl.BlockSpec((tm, tn), lambda i, j: (i, j)),
        out_shape=jax.ShapeDtypeStruct((m, d), F32),
        compiler_params=_cparams("parallel", "parallel"),
    )(pooled, w_pool, pool_scale.reshape(1, d), res, mod3)


def _pool_kernel(h_ref, o_ref, pad_ref, *, n, rows, tiles_per_group):
    zeros = jnp.zeros((SEQ_PAD, pad_ref.shape[1]), F32)
    pad_ref[0:SEQ_PAD, :] = zeros
    pad_ref[SEQ_PAD + n:2 * SEQ_PAD + n, :] = zeros
    for r0 in range(0, n, rows):
        pad_ref[SEQ_PAD + r0:SEQ_PAD + r0 + rows, :] = h_ref[r0:r0 + rows, :]
    group = pl.program_id(1) // tiles_per_group

    def pooled(win):
        left, right = win // 2, win - 1 - win // 2
        for r0 in range(0, n, rows):
            t = r0 + lax.broadcasted_iota(jnp.int32, (rows, 1), 0)
            acc = pad_ref[SEQ_PAD + r0 - left:SEQ_PAD + r0 - left + rows, :]
            for dlt in range(-left + 1, right + 1):
                acc = acc + pad_ref[SEQ_PAD + r0 + dlt:SEQ_PAD + r0 + dlt + rows, :]
            cnt = jnp.minimum(t + right + 1, n) - jnp.maximum(t - left, 0)
            mean = acc / cnt.astype(F32)
            o_ref[r0:r0 + rows, :] = (mean - h_ref[r0:r0 + rows, :]).astype(o_ref.dtype)

    for gi, win in enumerate(POOL_WINDOWS):
        assert win // 2 <= SEQ_PAD

        @pl.when(group == gi)
        def _(win=win):
            pooled(win)


def _pool(h, b, n):
    d = h.shape[1]
    ng = len(POOL_WINDOWS)
    gw = d // ng
    tc = _tile(gw, 256)
    rows = _tile(n, 1024, 8)
    tpg = gw // tc
    out = pl.pallas_call(
        functools.partial(_pool_kernel, n=n, rows=rows, tiles_per_group=tpg), name="pool",
        grid=(b, d // tc),
        in_specs=[pl.BlockSpec((None, n, tc), lambda i, j: (i, 0, j))],
        out_specs=pl.BlockSpec((None, None, n, tc), lambda i, j: (j // tpg, i, 0, j % tpg)),
        out_shape=jax.ShapeDtypeStruct((ng, b, n, gw), MXU_DTYPE),
        scratch_shapes=[pltpu.VMEM((n + 2 * SEQ_PAD, tc), F32)],
        compiler_params=_cparams("parallel", "parallel"),
    )(h.reshape(b, n, d))
    return out.reshape(ng, b * n, gw)


def _rope_tables(n_ctx, n_lat):
    ax = HEAD_DIM // 2
    inv = jnp.float32(ROPE_THETA) ** (-jnp.arange(0, ax, 2, dtype=F32) / ax)
    t = jnp.arange(n_lat)
    ang = jnp.concatenate([(t // GRID_W).astype(F32)[:, None] * inv,
                           (t % GRID_W).astype(F32)[:, None] * inv], axis=-1)
    cos = jnp.repeat(jnp.cos(ang), 2, axis=-1)
    sin = jnp.repeat(jnp.sin(ang), 2, axis=-1) * jnp.tile(jnp.array([-1.0, 1.0], F32), HEAD_DIM // 2)
    cos = jnp.concatenate([jnp.ones((n_ctx, HEAD_DIM), F32), cos], axis=0)
    sin = jnp.concatenate([jnp.zeros((n_ctx, HEAD_DIM), F32), sin], axis=0)
    return cos, sin


def _qk_prep_kernel(q_ref, k_ref, v_ref, cos_ref, sin_ref, qn_ref, kn_ref, qo_ref, kt_ref, vo_ref):
    cos, sin = cos_ref[...], sin_ref[...]
    even = (lax.broadcasted_iota(jnp.int32, cos.shape, 1) & 1) == 0

    def norm_rope(x, w):
        x = x * lax.rsqrt(jnp.mean(x * x, axis=-1, keepdims=True) + EPS) * w
        partner = jnp.where(even, pltpu.roll(x, HEAD_DIM - 1, 1), pltpu.roll(x, 1, 1))
        return x * cos + partner * sin

    scale = HEAD_DIM ** -0.5 * LOG2_E
    for h in range(q_ref.shape[1] // HEAD_DIM):
        sl = slice(h * HEAD_DIM, (h + 1) * HEAD_DIM)
        qo_ref[:, sl] = (norm_rope(q_ref[:, sl], qn_ref[...]) * scale).astype(qo_ref.dtype)
    ones = jnp.ones((v_ref.shape[0], HEAD_DIM), vo_ref.dtype)
    for h in range(k_ref.shape[1] // HEAD_DIM):
        sl = slice(h * HEAD_DIM, (h + 1) * HEAD_DIM)
        kt_ref[h] = norm_rope(k_ref[:, sl], kn_ref[...]).T.astype(kt_ref.dtype)
        vo_ref[:, 2 * h * HEAD_DIM:(2 * h + 1) * HEAD_DIM] = v_ref[:, sl].astype(vo_ref.dtype)
        vo_ref[:, (2 * h + 1) * HEAD_DIM:(2 * h + 2) * HEAD_DIM] = ones


def _qk_prep(p, wq, wk, cos, sin, q_norm, k_norm):
    b, s, _ = p.shape
    kvh = wk // HEAD_DIM
    tr = _tile(s, 512, 128)
    return pl.pallas_call(
        _qk_prep_kernel, name="qk_prep",
        grid=(b, s // tr),
        in_specs=[pl.BlockSpec((None, tr, wq), lambda i, t: (i, t, 0)),
                  pl.BlockSpec((None, tr, wk), lambda i, t: (i, t, wq // wk)),
                  pl.BlockSpec((None, tr, wk), lambda i, t: (i, t, wq // wk + 1)),
                  pl.BlockSpec((tr, HEAD_DIM), lambda i, t: (t, 0)),
                  pl.BlockSpec((tr, HEAD_DIM), lambda i, t: (t, 0)),
                  pl.BlockSpec((1, HEAD_DIM), lambda i, t: (0, 0)),
                  pl.BlockSpec((1, HEAD_DIM), lambda i, t: (0, 0))],
        out_specs=[pl.BlockSpec((None, tr, wq), lambda i, t: (i, t, 0)),
                   pl.BlockSpec((None, kvh, HEAD_DIM, tr), lambda i, t: (i, 0, 0, t)),
                   pl.BlockSpec((None, tr, 2 * wk), lambda i, t: (i, t, 0))],
        out_shape=[jax.ShapeDtypeStruct((b, s, wq), MXU_DTYPE),
                   jax.ShapeDtypeStruct((b, kvh, HEAD_DIM, s), MXU_DTYPE),
                   jax.ShapeDtypeStruct((b, s, 2 * wk), MXU_DTYPE)],
        compiler_params=_cparams("parallel", "parallel"),
    )(p, p, p, cos, sin, q_norm.reshape(1, HEAD_DIM), k_norm.reshape(1, HEAD_DIM))


def _attn_kernel(q_ref, kt_ref, v_ref, o_ref, *, kb):
    tq = q_ref.shape[0]
    nblk = kt_ref.shape[1] // kb
    q = jnp.concatenate([q_ref[:, g * HEAD_DIM:(g + 1) * HEAD_DIM] for g in range(GQA_GROUP)], axis=0)

    def scores(j):
        return _dot(q, kt_ref[:, j * kb:(j + 1) * kb])

    def weigh(j, s):
        m = jnp.max(s, axis=-1, keepdims=True)
        p = jnp.exp2(s - m).astype(v_ref.dtype)
        return m, _dot(p, v_ref[j * kb:(j + 1) * kb, :])

    parts = []
    s_next = scores(0)
    for j in range(nblk):
        s_cur = s_next
        if j + 1 < nblk:
            s_next = scores(j + 1)
        parts.append(weigh(j, s_cur))
    m_all = functools.reduce(jnp.maximum, [m for m, _ in parts])
    acc = functools.reduce(lambda a, c: a + c, [jnp.exp2(m - m_all) * o for m, o in parts])
    out = acc[:, :HEAD_DIM] / acc[:, HEAD_DIM:]
    for g in range(GQA_GROUP):
        o_ref[:, g * HEAD_DIM:(g + 1) * HEAD_DIM] = out[g * tq:(g + 1) * tq].astype(o_ref.dtype)


def _attention(q, kt, v_ext, n_ctx):
    b, s, wq = q.shape
    n = s - n_ctx
    kvh = kt.shape[1]
    gw = GQA_GROUP * HEAD_DIM
    tq = _tile(n_ctx, 128, 8)
    assert n % tq == 0
    kb = _tile(s, 1536)
    qoff = n_ctx // tq
    return pl.pallas_call(
        functools.partial(_attn_kernel, kb=kb), name="attention",
        grid=(b, kvh, n // tq),
        in_specs=[pl.BlockSpec((None, tq, gw), lambda i, h, qi: (i, qi + qoff, h)),
                  pl.BlockSpec((None, None, HEAD_DIM, s), lambda i, h, qi: (i, h, 0, 0)),
                  pl.BlockSpec((None, s, 2 * HEAD_DIM), lambda i, h, qi: (i, 0, h))],
        out_specs=pl.BlockSpec((None, tq, gw), lambda i, h, qi: (i, qi, h)),
        out_shape=jax.ShapeDtypeStruct((b, n, wq), MXU_DTYPE),
        compiler_params=_cparams("parallel", "parallel", "parallel"),
    )(q, kt, v_ext)


def _dn_prep_kernel(x_ref, w_ref, o_ref, pad_ref, *, s, n_ctx, rows, nh):
    j = pl.program_id(1)
    zeros = jnp.zeros((SEQ_PAD, HEAD_DIM), F32)
    pad_ref[0:SEQ_PAD, :] = zeros
    pad_ref[SEQ_PAD + s:2 * SEQ_PAD + s, :] = zeros
    for r0 in range(0, s, rows):
        pad_ref[SEQ_PAD + r0:SEQ_PAD + r0 + rows, :] = x_ref[r0:r0 + rows, :]
    w = w_ref[...]
    qk_scale = jnp.where(j < nh, HEAD_DIM ** -0.5, 1.0).astype(F32)
    for r0 in range(0, s, rows):
        t = r0 + lax.broadcasted_iota(jnp.int32, (rows, 1), 0)
        seg_lo = jnp.where(t < n_ctx, 0, n_ctx)
        seg_hi = jnp.where(t < n_ctx, n_ctx, s)
        acc = jnp.zeros((rows, HEAD_DIM), F32)
        for tap, dlt in enumerate(CONV_TAPS):
            xs = pad_ref[SEQ_PAD + r0 + dlt:SEQ_PAD + r0 + dlt + rows, :]
            valid = (t + dlt >= seg_lo) & (t + dlt < seg_hi)
            acc = acc + jnp.where(valid, xs, 0.0) * w[tap:tap + 1, :]
        y = acc * jax.nn.sigmoid(acc)
        nrm = lax.rsqrt(jnp.sum(y * y, axis=-1, keepdims=True) + EPS) * qk_scale
        o_ref[r0:r0 + rows, :] = y * jnp.where(j < 2 * nh, nrm, 1.0)


def _dn_prep(p, conv_w, col0, n_ctx, nh):
    b, s, _ = p.shape
    rows = _tile(s, 1536, 8)
    base = col0 // HEAD_DIM
    return pl.pallas_call(
        functools.partial(_dn_prep_kernel, s=s, n_ctx=n_ctx, rows=rows, nh=nh), name="dn_prep",
        grid=(b, 3 * nh),
        in_specs=[pl.BlockSpec((None, s, HEAD_DIM), lambda i, j: (i, 0, base + j)),
                  pl.BlockSpec((len(CONV_TAPS), HEAD_DIM), lambda i, j: (0, j))],
        out_specs=pl.BlockSpec((None, None, s, HEAD_DIM), lambda i, j: (i, j, 0, 0)),
        out_shape=jax.ShapeDtypeStruct((b, 3 * nh, s, HEAD_DIM), F32),
        scratch_shapes=[pltpu.VMEM((s + 2 * SEQ_PAD, HEAD_DIM), F32)],
        compiler_params=_cparams("parallel", "parallel"),
    )(p, conv_w)


def _chunk_masks(g, chunk, reverse):
    ri = lax.broadcasted_iota(jnp.int32, (g, g), 0)
    ci = lax.broadcasted_iota(jnp.int32, (g, g), 1)
    sh = chunk.bit_length() - 1
    same = lax.shift_right_logical(ri, sh) == lax.shift_right_logical(ci, sh)
    if reverse:
        return same & (ci >= ri), same & (ci > ri), ri, ci
    return same & (ci <= ri), same & (ci < ri), ri, ci


def _gate_kernel(raw_ref, alog_ref, dtb_ref, col_ref, row_ref, *, nh):
    raw = raw_ref[...]
    g_rows = raw.shape[0]
    lane = lax.broadcasted_iota(jnp.int32, raw.shape, 1)
    xs = raw + dtb_ref[...]
    softplus = jnp.maximum(xs, 0.0) + jnp.log1p(jnp.exp(-jnp.abs(xs)))
    gl = jnp.where(lane < 2 * nh, -jnp.exp(alog_ref[...]) * softplus, 0.0)
    beta = jax.nn.sigmoid(raw)
    incl_f, _, _, _ = _chunk_masks(g_rows, DN_CHUNK, False)
    incl_b, _, _, _ = _chunk_masks(g_rows, DN_CHUNK, True)
    cum_f = _dot_exact_lhs(incl_f.astype(MXU_DTYPE), gl)
    cum_b = _dot_exact_lhs(incl_b.astype(MXU_DTYPE), gl)
    total = cum_f + cum_b - gl
    col = jnp.where(lane < nh, cum_f,
                    jnp.where(lane < 2 * nh, cum_b,
                              jnp.where(lane < 4 * nh, beta, pltpu.roll(total, 4 * nh, 1))))
    col_ref[...] = col
    row_ref[...] = col.T


def _dn_gates(raw, a_log, dt_bias, nh):
    b, s, lanes = raw.shape
    assert 6 * nh <= lanes and s % DN_GROUP == 0
    pad = lambda v: jnp.pad(v.reshape(1, 2 * nh).astype(F32), ((0, 0), (0, lanes - 2 * nh)))
    return pl.pallas_call(
        functools.partial(_gate_kernel, nh=nh), name="dn_gates",
        grid=(b, s // DN_GROUP),
        in_specs=[pl.BlockSpec((None, DN_GROUP, lanes), lambda i, t: (i, t, 0)),
                  pl.BlockSpec((1, lanes), lambda i, t: (0, 0)),
                  pl.BlockSpec((1, lanes), lambda i, t: (0, 0))],
        out_specs=[pl.BlockSpec((None, DN_GROUP, lanes), lambda i, t: (i, t, 0)),
                   pl.BlockSpec((None, lanes, DN_GROUP), lambda i, t: (i, 0, t))],
        out_shape=[jax.ShapeDtypeStruct((b, s, lanes), F32),
                   jax.ShapeDtypeStruct((b, lanes, s), F32)],
        compiler_params=_cparams("parallel", "parallel"),
    )(raw, pad(a_log), pad(dt_bias))


def _unit_triangular_inverse_minus_eye(lmat, diag):
    def mm(a, b16):
        return _bdot(a.astype(MXU_DTYPE), b16)

    dm = jnp.where(diag, lmat, 0.0)
    off = lmat - dm
    inv_d = -dm
    power = dm
    span = 2
    while span < DN_DIAG:
        p16 = power.astype(MXU_DTYPE)
        power = _bdot(p16, p16)
        inv_d = inv_d + power + mm(inv_d, power.astype(MXU_DTYPE))
        span *= 2
    m_blk = off + mm(inv_d, off.astype(MXU_DTYPE))
    inv_m = -m_blk
    power = m_blk
    span = 2
    while span < DN_CHUNK // DN_DIAG:
        p16 = power.astype(MXU_DTYPE)
        power = _bdot(p16, p16)
        inv_m = inv_m + power + mm(inv_m, power.astype(MXU_DTYPE))
        span *= 2
    return inv_m + inv_d + mm(inv_m, inv_d.astype(MXU_DTYPE))


def _dn_chunk_kernel(q_ref, k_ref, v_ref, gcol_ref, grow_ref, u_ref, wq_ref, kq_ref, gt_ref, *, nh):
    hb = q_ref.shape[0]
    g_rows = q_ref.shape[1]
    n_chunks = g_rows // DN_CHUNK
    head0 = pl.program_id(1) * hb
    gcol = gcol_ref[...]
    lane = lax.broadcasted_iota(jnp.int32, gcol.shape, 1)

    def column(i):
        return jnp.sum(jnp.where(lane == i, gcol, 0.0), axis=1, keepdims=True)

    chains = [(d, hh) for d in (0, 1) for hh in range(hb)]
    idx = [d * nh + head0 + hh for d, hh in chains]
    gc = jnp.stack([column(i) for i in idx])
    beta = jnp.stack([column(2 * nh + i) for i in idx])
    total = jnp.stack([column(4 * nh + i) for i in idx])
    gr = jnp.stack([grow_ref[pl.ds(i, 1), :] for i in idx])
    incl_f, strict_f, ri, ci = _chunk_masks(g_rows, DN_CHUNK, False)
    incl_b, strict_b, _, _ = _chunk_masks(g_rows, DN_CHUNK, True)
    incl = jnp.stack([incl_b if d else incl_f for d, _ in chains])
    strict = jnp.stack([strict_b if d else strict_f for d, _ in chains])
    sh = DN_DIAG.bit_length() - 1
    diag = lax.shift_right_logical(ri, sh) == lax.shift_right_logical(ci, sh)

    q = jnp.concatenate([q_ref[...], q_ref[...]], axis=0)
    k = jnp.concatenate([k_ref[...], k_ref[...]], axis=0)
    v = jnp.concatenate([v_ref[...], v_ref[...]], axis=0)
    decay = jnp.exp(jnp.where(incl, gc - gr, -jnp.inf))
    kbeta = k * beta
    k16 = k.astype(MXU_DTYPE)
    lmat = jnp.where(strict, _bdot_nt(kbeta.astype(MXU_DTYPE), k16) * decay, 0.0)
    qk = _bdot_nt(q.astype(MXU_DTYPE), k16) * decay
    t_off = _unit_triangular_inverse_minus_eye(lmat, diag)
    eg = jnp.exp(gc)
    rhs = jnp.concatenate([v * beta, kbeta * eg], axis=2)
    sol = rhs + _bdot(t_off.astype(MXU_DTYPE), rhs.astype(MXU_DTYPE))
    u_ref[...] = sol[:, :, :HEAD_DIM].reshape(u_ref.shape)
    w16 = sol[:, :, HEAD_DIM:].astype(MXU_DTYPE)
    qd16 = (q * eg).astype(MXU_DTYPE)
    kd = k * jnp.exp(total - gc)
    kdt = jnp.stack([kd[c].T for c in range(len(chains))])
    n_c = len(chains)
    for cc in range(n_chunks):
        lo, hi = cc * DN_CHUNK, (cc + 1) * DN_CHUNK
        wq_ref[:, :, cc] = jnp.concatenate([w16[:, lo:hi], qd16[:, lo:hi]], axis=1).reshape(
            (2, hb, 2 * DN_CHUNK, HEAD_DIM))
        tl = slice((cc // 2) * HEAD_DIM, (cc // 2 + 1) * HEAD_DIM)
        both = jnp.concatenate([kdt[:, :, tl], qk[:, lo:hi, tl]], axis=1)
        if cc % 2:
            both = pltpu.roll(both.reshape(n_c * (HEAD_DIM + DN_CHUNK), HEAD_DIM), DN_CHUNK, 1).reshape(both.shape)
        kq_ref[:, :, cc] = both.astype(MXU_DTYPE).reshape((2, hb, HEAD_DIM + DN_CHUNK, HEAD_DIM))
    gts = [jnp.broadcast_to(jnp.exp(total[:, cc * DN_CHUNK:cc * DN_CHUNK + 1, :]), (n_c, 1, HEAD_DIM))
           for cc in range(n_chunks)]
    gts.append(jnp.zeros((n_c, 8 - n_chunks, HEAD_DIM), F32))
    gt_ref[...] = jnp.concatenate(gts, axis=1).reshape(gt_ref.shape)


def _dn_chunks(qkv, gcol, grow, nh):
    b, _, s, _ = qkv.shape
    lanes = gcol.shape[2]
    hb = min(DN_PREP_HEADS, nh)
    assert nh % hb == 0 and DN_GROUP // DN_CHUNK <= 8 and HEAD_DIM == 2 * DN_CHUNK
    nhb, nt, ncg = nh // hb, s // DN_GROUP, DN_GROUP // DN_CHUNK

    def qkv_spec(part):
        return pl.BlockSpec((None, hb, DN_GROUP, HEAD_DIM), lambda i, h, t: (i, part * nhb + h, t, 0))

    return pl.pallas_call(
        functools.partial(_dn_chunk_kernel, nh=nh), name="dn_chunks",
        grid=(b, nhb, nt),
        in_specs=[qkv_spec(0), qkv_spec(1), qkv_spec(2),
                  pl.BlockSpec((None, DN_GROUP, lanes), lambda i, h, t: (i, t, 0)),
                  pl.BlockSpec((None, lanes, DN_GROUP), lambda i, h, t: (i, 0, t))],
        out_specs=[pl.BlockSpec((None, 2, hb, DN_GROUP, HEAD_DIM), lambda i, h, t: (i, 0, h, t, 0)),
                   pl.BlockSpec((None, 2, hb, ncg, 2 * DN_CHUNK, HEAD_DIM), lambda i, h, t: (i, 0, h, t, 0, 0)),
                   pl.BlockSpec((None, 2, hb, ncg, HEAD_DIM + DN_CHUNK, HEAD_DIM),
                                lambda i, h, t: (i, 0, h, t, 0, 0)),
                   pl.BlockSpec((None, 2, hb, None, 8, HEAD_DIM), lambda i, h, t: (i, 0, h, t, 0, 0))],
        out_shape=[jax.ShapeDtypeStruct((b, 2, nh, s, HEAD_DIM), F32),
                   jax.ShapeDtypeStruct((b, 2, nh, s // DN_CHUNK, 2 * DN_CHUNK, HEAD_DIM), MXU_DTYPE),
                   jax.ShapeDtypeStruct((b, 2, nh, s // DN_CHUNK, HEAD_DIM + DN_CHUNK, HEAD_DIM), MXU_DTYPE),
                   jax.ShapeDtypeStruct((b, 2, nh, nt, 8, HEAD_DIM), F32)],
        compiler_params=_cparams("parallel", "parallel", "parallel"),
    )(qkv, qkv, qkv, gcol, grow)


def _dn_scan_kernel(uf, wqf, kqf, gtf, ub, wqb, kqb, gtb, of_ref, ob_ref, state_ref):
    hs = uf.shape[0]
    n_chunks = uf.shape[1] // DN_CHUNK

    @pl.when(pl.program_id(2) == 0)
    def _():
        state_ref[...] = jnp.zeros_like(state_ref)

    state = state_ref[...]
    zeros = jnp.zeros((2 * hs, HEAD_DIM - DN_CHUNK, HEAD_DIM), F32)
    for i in range(n_chunks):
        cf, cb = i, n_chunks - 1 - i
        lo_f, lo_b = cf * DN_CHUNK, cb * DN_CHUNK
        wq = jnp.concatenate([wqf[:, cf], wqb[:, cb]], axis=0)
        kq = jnp.concatenate([kqf[:, cf], kqb[:, cb]], axis=0)
        u = jnp.concatenate([uf[:, lo_f:lo_f + DN_CHUNK], ub[:, lo_b:lo_b + DN_CHUNK]], axis=0)
        gt = jnp.concatenate([gtf[:, cf:cf + 1], gtb[:, cb:cb + 1]], axis=0)
        a1 = _bdot(wq, state.astype(MXU_DTYPE))
        v_new = u - a1[:, :DN_CHUNK]
        v_pad = jnp.concatenate([v_new, zeros], axis=1).astype(MXU_DTYPE)
        a2 = _bdot(kq, v_pad)
        state = state * gt + a2[:, :HEAD_DIM]
        o = a1[:, DN_CHUNK:] + a2[:, HEAD_DIM:]
        of_ref[:, lo_f:lo_f + DN_CHUNK, :] = o[:hs]
        ob_ref[:, lo_b:lo_b + DN_CHUNK, :] = o[hs:]
    state_ref[...] = state


def _dn_scan(u, wq, kq, gt, n_ctx):
    b, _, nh, s, _ = u.shape
    assert n_ctx % DN_GROUP == 0 and s % DN_GROUP == 0
    nc, nt = n_ctx // DN_GROUP, s // DN_GROUP
    n = s - n_ctx
    hs = min(DN_SCAN_HEADS, nh)
    assert nh % hs == 0
    ncg = DN_GROUP // DN_CHUNK

    def fwd(t):
        return t

    def bwd(t):
        return jnp.where(t < nc, nc - 1 - t, nc + nt - 1 - t)

    def specs(d, pos):
        return [pl.BlockSpec((None, None, hs, DN_GROUP, HEAD_DIM), lambda i, h, t: (i, d, h, pos(t), 0)),
                pl.BlockSpec((None, None, hs, ncg, 2 * DN_CHUNK, HEAD_DIM), lambda i, h, t: (i, d, h, pos(t), 0, 0)),
                pl.BlockSpec((None, None, hs, ncg, HEAD_DIM + DN_CHUNK, HEAD_DIM),
                             lambda i, h, t: (i, d, h, pos(t), 0, 0)),
                pl.BlockSpec((None, None, hs, None, 8, HEAD_DIM), lambda i, h, t: (i, d, h, pos(t), 0, 0))]

    def out_spec(first, pos):
        return pl.BlockSpec((None, hs, DN_GROUP, HEAD_DIM),
                            lambda i, h, t: (i, h, jnp.where(t < nc, first, pos(t) - nc), 0))

    out = jax.ShapeDtypeStruct((b, nh, n, HEAD_DIM), F32)
    return pl.pallas_call(
        _dn_scan_kernel, name="dn_scan",
        grid=(b, nh // hs, nt),
        in_specs=specs(0, fwd) + specs(1, bwd),
        out_specs=[out_spec(0, fwd), out_spec(nt - 1 - nc, bwd)],
        out_shape=[out, out],
        scratch_shapes=[pltpu.VMEM((2 * hs, HEAD_DIM, HEAD_DIM), F32)],
        compiler_params=_cparams("parallel", "parallel", "arbitrary"),
    )(u, wq, kq, gt, u, wq, kq, gt)


def _gated_norm_kernel(of_ref, ob_ref, z_ref, w_ref, o_ref):
    w = w_ref[...]
    for h in range(of_ref.shape[0]):
        sl = slice(h * HEAD_DIM, (h + 1) * HEAD_DIM)
        o = of_ref[h] + ob_ref[h]
        z = z_ref[:, sl]
        y = o * lax.rsqrt(jnp.mean(o * o, axis=-1, keepdims=True) + EPS) * w * (z * jax.nn.sigmoid(z))
        o_ref[:, sl] = y.astype(o_ref.dtype)


def _gated_norm(o_f, o_b, p, z_col0, n_ctx, o_norm):
    b, nh, n, _ = o_f.shape
    hpb = nh // 2
    zw = hpb * HEAD_DIM
    assert nh % 2 == 0 and z_col0 % zw == 0
    tr = _tile(n_ctx, 256, 8)
    assert n % tr == 0
    roff = n_ctx // tr
    return pl.pallas_call(
        _gated_norm_kernel, name="gated_norm",
        grid=(b, n // tr, 2),
        in_specs=[pl.BlockSpec((None, hpb, tr, HEAD_DIM), lambda i, r, hf: (i, hf, r, 0)),
                  pl.BlockSpec((None, hpb, tr, HEAD_DIM), lambda i, r, hf: (i, hf, r, 0)),
                  pl.BlockSpec((None, tr, zw), lambda i, r, hf: (i, r + roff, z_col0 // zw + hf)),
                  pl.BlockSpec((1, HEAD_DIM), lambda i, r, hf: (0, 0))],
        out_specs=pl.BlockSpec((None, tr, zw), lambda i, r, hf: (i, r, hf)),
        out_shape=jax.ShapeDtypeStruct((b, n, nh * HEAD_DIM), MXU_DTYPE),
        compiler_params=_cparams("parallel", "parallel", "parallel"),
    )(o_f, o_b, p, o_norm.reshape(1, HEAD_DIM))


def _ffn(x, b, n, norm_g, mod3, w_gate, w_up, w_down):
    d = x.shape[1]
    f = w_gate.shape[1]
    fp = -(-f // FFN_ALIGN) * FFN_ALIGN
    wg = jnp.pad(w_gate.astype(MXU_DTYPE), ((0, 0), (0, fp - f)))
    wu = jnp.pad(w_up.astype(MXU_DTYPE), ((0, 0), (0, fp - f)))
    wd = jnp.pad(w_down.astype(MXU_DTYPE), ((0, fp - f), (0, 0)))
    h = _modnorm(x.reshape(b, n, d), norm_g, mod3, 3, 4, MXU_DTYPE)
    hid = _ffn_up(h, wg, wu)
    return _matmul_residual(hid, wd, x, mod3, 5, n, 2816)


def _hybrid_layer(x, ctx, c, c_ctx, w_mod, b_mod, norm_mix, norm_ffn, w_in, w_out, q_norm, k_norm,
                  conv_w, a_log, dt_bias, o_norm, w_gate, w_up, w_down, layer):
    b, n, d = x.shape
    n_ctx = ctx.shape[1]
    s = n_ctx + n
    wq = d // 2
    wk = wq // GQA_GROUP
    nh = wq // HEAD_DIM
    main = (wq + 2 * wk) + 4 * wq
    assert w_in.shape[1] == main + 4 * nh

    cond8 = jnp.zeros((8, d), F32).at[:b].set(c).at[b].set(c_ctx)
    mod3 = _mod_vectors(cond8, w_mod, b_mod, layer)
    h_all = _modnorm_seq(x, ctx, norm_mix, mod3, b, 0, 1).reshape(b * s, d)

    w_main = w_in[:, :main].astype(MXU_DTYPE)
    w_gates = jnp.pad(w_in[:, main:].astype(MXU_DTYPE), ((0, 0), (0, HEAD_DIM - 4 * nh)))
    p = _matmul(h_all, w_main, F32, 768, 1024).reshape(b, s, main)
    raw_gates = _matmul(h_all, w_gates, F32, 768, HEAD_DIM).reshape(b, s, HEAD_DIM)

    cos, sin = _rope_tables(n_ctx, n)
    qa, kta, va = _qk_prep(p, wq, wk, cos, sin, q_norm, k_norm)
    ya = _attention(qa, kta, va, n_ctx)

    qkv = _dn_prep(p, conv_w, wq + 2 * wk, n_ctx, nh)
    gcol, grow = _dn_gates(raw_gates, a_log, dt_bias, nh)
    u, wqc, kqc, gt = _dn_chunks(qkv, gcol, grow, nh)
    o_f, o_b = _dn_scan(u, wqc, kqc, gt, n_ctx)
    yb = _gated_norm(o_f, o_b, p, wq + 2 * wk + 3 * wq, n_ctx, o_norm)

    x1 = _out_proj_residual(ya.reshape(b * n, wq), yb.reshape(b * n, wq), w_out.astype(MXU_DTYPE),
                            x.reshape(b * n, d), mod3, 2, n)
    return _ffn(x1, b, n, norm_ffn, mod3, w_gate, w_up, w_down)


def _pool_layer(x, b, n, c, w_mod, b_mod, norm_mix, norm_ffn, w_pool, pool_scale, w_gate, w_up, w_down, layer):
    d = x.shape[1]
    cond8 = jnp.zeros((8, d), F32).at[:b].set(c)
    mod3 = _mod_vectors(cond8, w_mod, b_mod, layer)
    h = _modnorm(x.reshape(b, n, d), norm_mix, mod3, 0, 1, F32)
    pooled = _pool(h, b, n)
    x1 = _pool_proj_residual(pooled, w_pool.astype(MXU_DTYPE), pool_scale, x, mod3, 2, n)
    return _ffn(x1, b, n, norm_ffn, mod3, w_gate, w_up, w_down)


def kernel(x, c, ctx, c_ctx, w_mod, b_mod, norm_mix, norm_ffn, w_in, w_out, q_norm, k_norm, conv_w, a_log,
           dt_bias, o_norm, w_pool, pool_scale, w_gate, w_up, w_down):
    b, n, d = x.shape
    depth = w_mod.shape[0]
    assert depth == 2
    xf = _hybrid_layer(x, ctx, c, c_ctx, w_mod, b_mod, norm_mix[0], norm_ffn[0], w_in[0], w_out[0], q_norm[0],
                       k_norm[0], conv_w[0], a_log[0], dt_bias[0], o_norm[0], w_gate[0], w_up[0], w_down[0], 0)
    xf = _pool_layer(xf, b, n, c, w_mod, b_mod, norm_mix[1], norm_ffn[1], w_pool[0], pool_scale[0],
                     w_gate[1], w_up[1], w_down[1], 1)
    return xf.reshape(b, n, d)
```

```python
import functools

import jax
import jax.numpy as jnp
from jax import lax
from jax.experimental import pallas as pl
from jax.experimental.pallas import tpu as pltpu

F32 = jnp.float32
MXU_DTYPE = jnp.bfloat16

EPS = 1e-6
HEAD_DIM = 128
GQA_GROUP = 4
GRID_W = 64
ROPE_THETA = 10000.0
LOG2_E = 1.4426950408889634
CONV_TAPS = (-2, -1, 0, 1)
DN_CHUNK = 64
DN_GROUP = 256
DN_DIAG = 16
DN_PREP_HEADS = 4
DN_SCAN_HEADS = 8
POOL_WINDOWS = (2, 4, 8, 16)
SEQ_PAD = 8
FFN_ALIGN = 1024
V7X_VMEM_LIMIT = 56 * 1024 * 1024


def _tile(dim, pref, mult=128):
    if dim <= pref:
        return dim
    t = (pref // mult) * mult
    while t >= mult:
        if dim % t == 0:
            return t
        t -= mult
    raise ValueError(f"no tile for {dim} (pref {pref}, mult {mult})")


def _cparams(*sem):
    return pltpu.CompilerParams(dimension_semantics=sem, vmem_limit_bytes=V7X_VMEM_LIMIT)


def _dot(a, b):
    return jnp.dot(a, b, preferred_element_type=F32)


def _bdot(a, b):
    return jnp.einsum("cik,ckj->cij", a, b, preferred_element_type=F32)


def _bdot_nt(a, b):
    return jnp.einsum("cid,cjd->cij", a, b, preferred_element_type=F32)


def _dot_exact_lhs(m, x):
    x1 = x.astype(MXU_DTYPE)
    r1 = x - x1.astype(F32)
    x2 = r1.astype(MXU_DTYPE)
    x3 = (r1 - x2.astype(F32)).astype(MXU_DTYPE)
    return _dot(m, x1) + _dot(m, x2) + _dot(m, x3)


def _mod_kernel(s_ref, w_ref, b_ref, o_ref):
    s = s_ref[...]
    s = s * jax.nn.sigmoid(s)
    o_ref[...] = _dot(s.astype(MXU_DTYPE), w_ref[...].astype(MXU_DTYPE)) + b_ref[...]


def _mod_vectors(cond8, w_mod, b_mod, layer):
    d, n6 = w_mod.shape[1], w_mod.shape[2]
    tn = _tile(n6, 512)
    out = pl.pallas_call(
        _mod_kernel, name="mod_vectors",
        grid=(n6 // tn,),
        in_specs=[pl.BlockSpec((8, d), lambda j: (0, 0)),
                  pl.BlockSpec((None, d, tn), lambda j: (layer, 0, j)),
                  pl.BlockSpec((None, 1, tn), lambda j: (layer, 0, j))],
        out_specs=pl.BlockSpec((8, tn), lambda j: (0, j)),
        out_shape=jax.ShapeDtypeStruct((8, n6), F32),
        compiler_params=_cparams("parallel"),
    )(cond8, w_mod, b_mod.reshape(b_mod.shape[0], 1, n6))
    return out.reshape(8, 1, n6)


def _modulate_rows(x, g, sc, sh):
    r = lax.rsqrt(jnp.mean(x * x, axis=-1, keepdims=True) + EPS)
    return x * r * (g * (1.0 + sc)) + sh


def _modnorm_kernel(x_ref, g_ref, sc_ref, sh_ref, o_ref):
    o_ref[...] = _modulate_rows(x_ref[...], g_ref[...], sc_ref[...], sh_ref[...]).astype(o_ref.dtype)


def _modnorm(x, g, mod3, k_shift, k_scale, out_dtype):
    b, n, d = x.shape
    tr = _tile(n, 256, 8)
    return pl.pallas_call(
        _modnorm_kernel, name="modnorm",
        grid=(b, n // tr),
        in_specs=[pl.BlockSpec((None, tr, d), lambda i, t: (i, t, 0)),
                  pl.BlockSpec((1, d), lambda i, t: (0, 0)),
                  pl.BlockSpec((None, 1, d), lambda i, t: (i, 0, k_scale)),
                  pl.BlockSpec((None, 1, d), lambda i, t: (i, 0, k_shift))],
        out_specs=pl.BlockSpec((None, tr, d), lambda i, t: (i, t, 0)),
        out_shape=jax.ShapeDtypeStruct((b, n, d), out_dtype),
        compiler_params=_cparams("parallel", "parallel"),
    )(x, g.reshape(1, d), mod3, mod3).reshape(b * n, d)


def _modnorm_seq_kernel(x_ref, c_ref, g_ref, sc_ref, sh_ref, o_ref, *, n_ctx_tiles):
    is_ctx = pl.program_id(1) < n_ctx_tiles
    x = jnp.where(is_ctx, c_ref[...], x_ref[...])
    o_ref[...] = _modulate_rows(x, g_ref[...], sc_ref[...], sh_ref[...]).astype(o_ref.dtype)


def _modnorm_seq(x, ctx, g, mod3, ctx_row, k_shift, k_scale):
    b, n, d = x.shape
    c = ctx.shape[1]
    tr = _tile(c, 256, 8)
    assert n % tr == 0
    nct = c // tr

    def mod_row(i, t):
        return jnp.where(t < nct, ctx_row, i)

    return pl.pallas_call(
        functools.partial(_modnorm_seq_kernel, n_ctx_tiles=nct), name="modnorm_seq",
        grid=(b, (c + n) // tr),
        in_specs=[pl.BlockSpec((None, tr, d), lambda i, t: (i, jnp.maximum(t - nct, 0), 0)),
                  pl.BlockSpec((None, tr, d), lambda i, t: (i, jnp.minimum(t, nct - 1), 0)),
                  pl.BlockSpec((1, d), lambda i, t: (0, 0)),
                  pl.BlockSpec((None, 1, d), lambda i, t: (mod_row(i, t), 0, k_scale)),
                  pl.BlockSpec((None, 1, d), lambda i, t: (mod_row(i, t), 0, k_shift))],
        out_specs=pl.BlockSpec((None, tr, d), lambda i, t: (i, t, 0)),
        out_shape=jax.ShapeDtypeStruct((b, c + n, d), MXU_DTYPE),
        compiler_params=_cparams("parallel", "parallel"),
    )(x, ctx, g.reshape(1, d), mod3, mod3)


def _mm_kernel(a_ref, w_ref, o_ref):
    o_ref[...] = _dot(a_ref[...], w_ref[...]).astype(o_ref.dtype)


def _matmul(a, w, out_dtype, tm_pref=1024, tn_pref=1024):
    m, k = a.shape
    n = w.shape[1]
    tm, tn = _tile(m, tm_pref, 8), _tile(n, tn_pref)
    return pl.pallas_call(
        _mm_kernel, name="matmul",
        grid=(m // tm, n // tn),
        in_specs=[pl.BlockSpec((tm, k), lambda i, j: (i, 0)),
                  pl.BlockSpec((k, tn), lambda i, j: (0, j))],
        out_specs=pl.BlockSpec((tm, tn), lambda i, j: (i, j)),
        out_shape=jax.ShapeDtypeStruct((m, n), out_dtype),
        compiler_params=_cparams("parallel", "parallel"),
    )(a, w)


def _ffn_up_kernel(a_ref, wg_ref, wu_ref, o_ref):
    a = a_ref[...]
    gate = _dot(a, wg_ref[...])
    up = _dot(a, wu_ref[...])
    o_ref[...] = (gate * jax.nn.sigmoid(gate) * up).astype(o_ref.dtype)


def _ffn_up(a, wg, wu):
    m, k = a.shape
    f = wg.shape[1]
    tm, tn = _tile(m, 1024, 8), _tile(f, 512)
    return pl.pallas_call(
        _ffn_up_kernel, name="ffn_up",
        grid=(m // tm, f // tn),
        in_specs=[pl.BlockSpec((tm, k), lambda i, j: (i, 0)),
                  pl.BlockSpec((k, tn), lambda i, j: (0, j)),
                  pl.BlockSpec((k, tn), lambda i, j: (0, j))],
        out_specs=pl.BlockSpec((tm, tn), lambda i, j: (i, j)),
        out_shape=jax.ShapeDtypeStruct((m, f), MXU_DTYPE),
        compiler_params=_cparams("parallel", "parallel"),
    )(a, wg, wu)


def _mm_res_kernel(a_ref, w_ref, r_ref, g_ref, o_ref, acc_ref):
    kk = pl.program_id(2)

    @pl.when(kk == 0)
    def _():
        acc_ref[...] = jnp.zeros_like(acc_ref)

    acc_ref[...] += _dot(a_ref[...], w_ref[...])

    @pl.when(kk == pl.num_programs(2) - 1)
    def _():
        o_ref[...] = r_ref[...] + g_ref[...] * acc_ref[...]


def _matmul_residual(a, w, res, mod3, k_gate, rows_per_sample, tk_pref):
    m, k = a.shape
    n = w.shape[1]
    tm, tn, tk = _tile(rows_per_sample, 1024, 8), _tile(n, 1024), _tile(k, tk_pref)
    tps = rows_per_sample // tm
    return pl.pallas_call(
        _mm_res_kernel, name="ffn_down_residual",
        grid=(m // tm, n // tn, k // tk),
        in_specs=[pl.BlockSpec((tm, tk), lambda i, j, kk: (i, kk)),
                  pl.BlockSpec((tk, tn), lambda i, j, kk: (kk, j)),
                  pl.BlockSpec((tm, tn), lambda i, j, kk: (i, j)),
                  pl.BlockSpec((None, 1, tn), lambda i, j, kk: (i // tps, 0, k_gate * (n // tn) + j))],
        out_specs=pl.BlockSpec((tm, tn), lambda i, j, kk: (i, j)),
        out_shape=jax.ShapeDtypeStruct((m, n), F32),
        scratch_shapes=[pltpu.VMEM((tm, tn), F32)],
        compiler_params=_cparams("parallel", "parallel", "arbitrary"),
    )(a, w, res, mod3)


def _mm2_res_kernel(a1_ref, a2_ref, w1_ref, w2_ref, r_ref, g_ref, o_ref):
    acc = _dot(a1_ref[...], w1_ref[...]) + _dot(a2_ref[...], w2_ref[...])
    o_ref[...] = r_ref[...] + g_ref[...] * acc


def _out_proj_residual(a1, a2, w, res, mod3, k_gate, rows_per_sample):
    m, kh = a1.shape
    n = w.shape[1]
    assert w.shape[0] == 2 * kh
    tm, tn = _tile(rows_per_sample, 1024, 8), _tile(n, 512)
    tps = rows_per_sample // tm
    return pl.pallas_call(
        _mm2_res_kernel, name="out_proj_residual",
        grid=(m // tm, n // tn),
        in_specs=[pl.BlockSpec((tm, kh), lambda i, j: (i, 0)),
                  pl.BlockSpec((tm, kh), lambda i, j: (i, 0)),
                  pl.BlockSpec((kh, tn), lambda i, j: (0, j)),
                  pl.BlockSpec((kh, tn), lambda i, j: (1, j)),
                  pl.BlockSpec((tm, tn), lambda i, j: (i, j)),
                  pl.BlockSpec((None, 1, tn), lambda i, j: (i // tps, 0, k_gate * (n // tn) + j))],
        out_specs=pl.BlockSpec((tm, tn), lambda i, j: (i, j)),
        out_shape=jax.ShapeDtypeStruct((m, n), F32),
        compiler_params=_cparams("parallel", "parallel"),
    )(a1, a2, w, w, res, mod3)


def _pool_mm_res_kernel(a_ref, w_ref, ps_ref, r_ref, g_ref, o_ref):
    o_ref[...] = r_ref[...] + g_ref[...] * (_dot(a_ref[...], w_ref[...]) * ps_ref[...])


def _pool_proj_residual(pooled, w_pool, pool_scale, res, mod3, k_gate, rows_per_sample):
    ng, m, gw = pooled.shape
    d = ng * gw
    tm, tn = _tile(rows_per_sample, 1024, 8), _tile(gw, 1024)
    tpg = gw // tn
    tps = rows_per_sample // tm
    return pl.pallas_call(
        _pool_mm_res_kernel, name="pool_proj_residual",
        grid=(m // tm, d // tn),
        in_specs=[pl.BlockSpec((None, tm, gw), lambda i, j: (j // tpg, i, 0)),
                  pl.BlockSpec((None, gw, tn), lambda i, j: (j // tpg, 0, j % tpg)),
                  pl.BlockSpec((1, tn), lambda i, j: (0, j)),
                  pl.BlockSpec((tm, tn), lambda i, j: (i, j)),
                  pl.BlockSpec((None, 1, tn), lambda i, j: (i // tps, 0, k_gate * (d // tn) + j))],
        out_specs=pl.BlockSpec((tm, tn), lambda i, j: (i, j)),
        out_shape=jax.ShapeDtypeStruct((m, d), F32),
        compiler_params=_cparams("parallel", "parallel"),
    )(pooled, w_pool, pool_scale.reshape(1, d), res, mod3)


def _pool_kernel(h_ref, o_ref, pad_ref, *, n, rows, tiles_per_group):
    zeros = jnp.zeros((SEQ_PAD, pad_ref.shape[1]), F32)
    pad_ref[0:SEQ_PAD, :] = zeros
    pad_ref[SEQ_PAD + n:2 * SEQ_PAD + n, :] = zeros
    for r0 in range(0, n, rows):
        pad_ref[SEQ_PAD + r0:SEQ_PAD + r0 + rows, :] = h_ref[r0:r0 + rows, :]
    group = pl.program_id(1) // tiles_per_group

    def pooled(win):
        left, right = win // 2, win - 1 - win // 2
        for r0 in range(0, n, rows):
            t = r0 + lax.broadcasted_iota(jnp.int32, (rows, 1), 0)
            acc = pad_ref[SEQ_PAD + r0 - left:SEQ_PAD + r0 - left + rows, :]
            for dlt in range(-left + 1, right + 1):
                acc = acc + pad_ref[SEQ_PAD + r0 + dlt:SEQ_PAD + r0 + dlt + rows, :]
            cnt = jnp.minimum(t + right + 1, n) - jnp.maximum(t - left, 0)
            mean = acc / cnt.astype(F32)
            o_ref[r0:r0 + rows, :] = (mean - h_ref[r0:r0 + rows, :]).astype(o_ref.dtype)

    for gi, win in enumerate(POOL_WINDOWS):
        assert win // 2 <= SEQ_PAD

        @pl.when(group == gi)
        def _(win=win):
            pooled(win)


def _pool(h, b, n):
    d = h.shape[1]
    ng = len(POOL_WINDOWS)
    gw = d // ng
    tc = _tile(gw, 256)
    rows = _tile(n, 1024, 8)
    tpg = gw // tc
    out = pl.pallas_call(
        functools.partial(_pool_kernel, n=n, rows=rows, tiles_per_group=tpg), name="pool",
        grid=(b, d // tc),
        in_specs=[pl.BlockSpec((None, n, tc), lambda i, j: (i, 0, j))],
        out_specs=pl.BlockSpec((None, None, n, tc), lambda i, j: (j // tpg, i, 0, j % tpg)),
        out_shape=jax.ShapeDtypeStruct((ng, b, n, gw), MXU_DTYPE),
        scratch_shapes=[pltpu.VMEM((n + 2 * SEQ_PAD, tc), F32)],
        compiler_params=_cparams("parallel", "parallel"),
    )(h.reshape(b, n, d))
    return out.reshape(ng, b * n, gw)


def _rope_tables(n_ctx, n_lat):
    ax = HEAD_DIM // 2
    inv = jnp.float32(ROPE_THETA) ** (-jnp.arange(0, ax, 2, dtype=F32) / ax)
    t = jnp.arange(n_lat)
    ang = jnp.concatenate([(t // GRID_W).astype(F32)[:, None] * inv,
                           (t % GRID_W).astype(F32)[:, None] * inv], axis=-1)
    cos = jnp.repeat(jnp.cos(ang), 2, axis=-1)
    sin = jnp.repeat(jnp.sin(ang), 2, axis=-1) * jnp.tile(jnp.array([-1.0, 1.0], F32), HEAD_DIM // 2)
    cos = jnp.concatenate([jnp.ones((n_ctx, HEAD_DIM), F32), cos], axis=0)
    sin = jnp.concatenate([jnp.zeros((n_ctx, HEAD_DIM), F32), sin], axis=0)
    return cos, sin


def _qk_prep_kernel(q_ref, k_ref, v_ref, cos_ref, sin_ref, qn_ref, kn_ref, qo_ref, kt_ref, vo_ref):
    cos, sin = cos_ref[...], sin_ref[...]
    even = (lax.broadcasted_iota(jnp.int32, cos.shape, 1) & 1) == 0

    def norm_rope(x, w):
        x = x * lax.rsqrt(jnp.mean(x * x, axis=-1, keepdims=True) + EPS) * w
        partner = jnp.where(even, pltpu.roll(x, HEAD_DIM - 1, 1), pltpu.roll(x, 1, 1))
        return x * cos + partner * sin

    scale = HEAD_DIM ** -0.5 * LOG2_E
    for h in range(q_ref.shape[1] // HEAD_DIM):
        sl = slice(h * HEAD_DIM, (h + 1) * HEAD_DIM)
        qo_ref[:, sl] = (norm_rope(q_ref[:, sl], qn_ref[...]) * scale).astype(qo_ref.dtype)
    ones = jnp.ones((v_ref.shape[0], HEAD_DIM), vo_ref.dtype)
    for h in range(k_ref.shape[1] // HEAD_DIM):
        sl = slice(h * HEAD_DIM, (h + 1) * HEAD_DIM)
        kt_ref[h] = norm_rope(k_ref[:, sl], kn_ref[...]).T.astype(kt_ref.dtype)
        vo_ref[:, 2 * h * HEAD_DIM:(2 * h + 1) * HEAD_DIM] = v_ref[:, sl].astype(vo_ref.dtype)
        vo_ref[:, (2 * h + 1) * HEAD_DIM:(2 * h + 2) * HEAD_DIM] = ones


def _qk_prep(p, wq, wk, cos, sin, q_norm, k_norm):
    b, s, _ = p.shape
    kvh = wk // HEAD_DIM
    tr = _tile(s, 512, 128)
    return pl.pallas_call(
        _qk_prep_kernel, name="qk_prep",
        grid=(b, s // tr),
        in_specs=[pl.BlockSpec((None, tr, wq), lambda i, t: (i, t, 0)),
                  pl.BlockSpec((None, tr, wk), lambda i, t: (i, t, wq // wk)),
                  pl.BlockSpec((None, tr, wk), lambda i, t: (i, t, wq // wk + 1)),
                  pl.BlockSpec((tr, HEAD_DIM), lambda i, t: (t, 0)),
                  pl.BlockSpec((tr, HEAD_DIM), lambda i, t: (t, 0)),
                  pl.BlockSpec((1, HEAD_DIM), lambda i, t: (0, 0)),
                  pl.BlockSpec((1, HEAD_DIM), lambda i, t: (0, 0))],
        out_specs=[pl.BlockSpec((None, tr, wq), lambda i, t: (i, t, 0)),
                   pl.BlockSpec((None, kvh, HEAD_DIM, tr), lambda i, t: (i, 0, 0, t)),
                   pl.BlockSpec((None, tr, 2 * wk), lambda i, t: (i, t, 0))],
        out_shape=[jax.ShapeDtypeStruct((b, s, wq), MXU_DTYPE),
                   jax.ShapeDtypeStruct((b, kvh, HEAD_DIM, s), MXU_DTYPE),
                   jax.ShapeDtypeStruct((b, s, 2 * wk), MXU_DTYPE)],
        compiler_params=_cparams("parallel", "parallel"),
    )(p, p, p, cos, sin, q_norm.reshape(1, HEAD_DIM), k_norm.reshape(1, HEAD_DIM))


def _attn_kernel(q_ref, kt_ref, v_ref, o_ref, *, kb):
    tq = q_ref.shape[0]
    nblk = kt_ref.shape[1] // kb
    q = jnp.concatenate([q_ref[:, g * HEAD_DIM:(g + 1) * HEAD_DIM] for g in range(GQA_GROUP)], axis=0)

    def scores(j):
        return _dot(q, kt_ref[:, j * kb:(j + 1) * kb])

    def weigh(j, s):
        m = jnp.max(s, axis=-1, keepdims=True)
        p = jnp.exp2(s - m).astype(v_ref.dtype)
        return m, _dot(p, v_ref[j * kb:(j + 1) * kb, :])

    parts = []
    s_next = scores(0)
    for j in range(nblk):
        s_cur = s_next
        if j + 1 < nblk:
            s_next = scores(j + 1)
        parts.append(weigh(j, s_cur))
    m_all = functools.reduce(jnp.maximum, [m for m, _ in parts])
    acc = functools.reduce(lambda a, c: a + c, [jnp.exp2(m - m_all) * o for m, o in parts])
    out = acc[:, :HEAD_DIM] / acc[:, HEAD_DIM:]
    for g in range(GQA_GROUP):
        o_ref[:, g * HEAD_DIM:(g + 1) * HEAD_DIM] = out[g * tq:(g + 1) * tq].astype(o_ref.dtype)


def _attention(q, kt, v_ext, n_ctx):
    b, s, wq = q.shape
    n = s - n_ctx
    kvh = kt.shape[1]
    gw = GQA_GROUP * HEAD_DIM
    tq = _tile(n_ctx, 256, 8)
    assert n % tq == 0
    kb = _tile(s, 1536)
    qoff = n_ctx // tq
    return pl.pallas_call(
        functools.partial(_attn_kernel, kb=kb), name="attention",
        grid=(b, kvh, n // tq),
        in_specs=[pl.BlockSpec((None, tq, gw), lambda i, h, qi: (i, qi + qoff, h)),
                  pl.BlockSpec((None, None, HEAD_DIM, s), lambda i, h, qi: (i, h, 0, 0)),
                  pl.BlockSpec((None, s, 2 * HEAD_DIM), lambda i, h, qi: (i, 0, h))],
        out_specs=pl.BlockSpec((None, tq, gw), lambda i, h, qi: (i, qi, h)),
        out_shape=jax.ShapeDtypeStruct((b, n, wq), MXU_DTYPE),
        compiler_params=_cparams("parallel", "parallel", "parallel"),
    )(q, kt, v_ext)


def _dn_prep_kernel(x_ref, w_ref, o_ref, pad_ref, *, s, n_ctx, rows, nh):
    j = pl.program_id(1)
    zeros = jnp.zeros((SEQ_PAD, HEAD_DIM), F32)
    pad_ref[0:SEQ_PAD, :] = zeros
    pad_ref[SEQ_PAD + s:2 * SEQ_PAD + s, :] = zeros
    for r0 in range(0, s, rows):
        pad_ref[SEQ_PAD + r0:SEQ_PAD + r0 + rows, :] = x_ref[r0:r0 + rows, :]
    w = w_ref[...]
    qk_scale = jnp.where(j < nh, HEAD_DIM ** -0.5, 1.0).astype(F32)
    for r0 in range(0, s, rows):
        t = r0 + lax.broadcasted_iota(jnp.int32, (rows, 1), 0)
        seg_lo = jnp.where(t < n_ctx, 0, n_ctx)
        seg_hi = jnp.where(t < n_ctx, n_ctx, s)
        acc = jnp.zeros((rows, HEAD_DIM), F32)
        for tap, dlt in enumerate(CONV_TAPS):
            xs = pad_ref[SEQ_PAD + r0 + dlt:SEQ_PAD + r0 + dlt + rows, :]
            valid = (t + dlt >= seg_lo) & (t + dlt < seg_hi)
            acc = acc + jnp.where(valid, xs, 0.0) * w[tap:tap + 1, :]
        y = acc * jax.nn.sigmoid(acc)
        nrm = lax.rsqrt(jnp.sum(y * y, axis=-1, keepdims=True) + EPS) * qk_scale
        o_ref[r0:r0 + rows, :] = y * jnp.where(j < 2 * nh, nrm, 1.0)


def _dn_prep(p, conv_w, col0, n_ctx, nh):
    b, s, _ = p.shape
    rows = _tile(s, 1536, 8)
    base = col0 // HEAD_DIM
    return pl.pallas_call(
        functools.partial(_dn_prep_kernel, s=s, n_ctx=n_ctx, rows=rows, nh=nh), name="dn_prep",
        grid=(b, 3 * nh),
        in_specs=[pl.BlockSpec((None, s, HEAD_DIM), lambda i, j: (i, 0, base + j)),
                  pl.BlockSpec((len(CONV_TAPS), HEAD_DIM), lambda i, j: (0, j))],
        out_specs=pl.BlockSpec((None, None, s, HEAD_DIM), lambda i, j: (i, j, 0, 0)),
        out_shape=jax.ShapeDtypeStruct((b, 3 * nh, s, HEAD_DIM), F32),
        scratch_shapes=[pltpu.VMEM((s + 2 * SEQ_PAD, HEAD_DIM), F32)],
        compiler_params=_cparams("parallel", "parallel"),
    )(p, conv_w)


def _chunk_masks(g, chunk, reverse):
    ri = lax.broadcasted_iota(jnp.int32, (g, g), 0)
    ci = lax.broadcasted_iota(jnp.int32, (g, g), 1)
    sh = chunk.bit_length() - 1
    same = lax.shift_right_logical(ri, sh) == lax.shift_right_logical(ci, sh)
    if reverse:
        return same & (ci >= ri), same & (ci > ri), ri, ci
    return same & (ci <= ri), same & (ci < ri), ri, ci


def _gate_kernel(raw_ref, alog_ref, dtb_ref, col_ref, row_ref, *, nh):
    raw = raw_ref[...]
    g_rows = raw.shape[0]
    lane = lax.broadcasted_iota(jnp.int32, raw.shape, 1)
    xs = raw + dtb_ref[...]
    softplus = jnp.maximum(xs, 0.0) + jnp.log1p(jnp.exp(-jnp.abs(xs)))
    gl = jnp.where(lane < 2 * nh, -jnp.exp(alog_ref[...]) * softplus, 0.0)
    beta = jax.nn.sigmoid(raw)
    incl_f, _, _, _ = _chunk_masks(g_rows, DN_CHUNK, False)
    incl_b, _, _, _ = _chunk_masks(g_rows, DN_CHUNK, True)
    cum_f = _dot_exact_lhs(incl_f.astype(MXU_DTYPE), gl)
    cum_b = _dot_exact_lhs(incl_b.astype(MXU_DTYPE), gl)
    total = cum_f + cum_b - gl
    col = jnp.where(lane < nh, cum_f,
                    jnp.where(lane < 2 * nh, cum_b,
                              jnp.where(lane < 4 * nh, beta, pltpu.roll(total, 4 * nh, 1))))
    col_ref[...] = col
    row_ref[...] = col.T


def _dn_gates(raw, a_log, dt_bias, nh):
    b, s, lanes = raw.shape
    assert 6 * nh <= lanes and s % DN_GROUP == 0
    pad = lambda v: jnp.pad(v.reshape(1, 2 * nh).astype(F32), ((0, 0), (0, lanes - 2 * nh)))
    return pl.pallas_call(
        functools.partial(_gate_kernel, nh=nh), name="dn_gates",
        grid=(b, s // DN_GROUP),
        in_specs=[pl.BlockSpec((None, DN_GROUP, lanes), lambda i, t: (i, t, 0)),
                  pl.BlockSpec((1, lanes), lambda i, t: (0, 0)),
                  pl.BlockSpec((1, lanes), lambda i, t: (0, 0))],
        out_specs=[pl.BlockSpec((None, DN_GROUP, lanes), lambda i, t: (i, t, 0)),
                   pl.BlockSpec((None, lanes, DN_GROUP), lambda i, t: (i, 0, t))],
        out_shape=[jax.ShapeDtypeStruct((b, s, lanes), F32),
                   jax.ShapeDtypeStruct((b, lanes, s), F32)],
        compiler_params=_cparams("parallel", "parallel"),
    )(raw, pad(a_log), pad(dt_bias))


def _unit_triangular_inverse_minus_eye(lmat, diag):
    def mm(a, b16):
        return _bdot(a.astype(MXU_DTYPE), b16)

    dm = jnp.where(diag, lmat, 0.0)
    off = lmat - dm
    inv_d = -dm
    power = dm
    span = 2
    while span < DN_DIAG:
        p16 = power.astype(MXU_DTYPE)
        power = _bdot(p16, p16)
        inv_d = inv_d + power + mm(inv_d, power.astype(MXU_DTYPE))
        span *= 2
    m_blk = off + mm(inv_d, off.astype(MXU_DTYPE))
    inv_m = -m_blk
    power = m_blk
    span = 2
    while span < DN_CHUNK // DN_DIAG:
        p16 = power.astype(MXU_DTYPE)
        power = _bdot(p16, p16)
        inv_m = inv_m + power + mm(inv_m, power.astype(MXU_DTYPE))
        span *= 2
    return inv_m + inv_d + mm(inv_m, inv_d.astype(MXU_DTYPE))


def _dn_chunk_kernel(q_ref, k_ref, v_ref, gcol_ref, grow_ref, u_ref, wq_ref, kq_ref, gt_ref, *, nh):
    hb = q_ref.shape[0]
    g_rows = q_ref.shape[1]
    n_chunks = g_rows // DN_CHUNK
    head0 = pl.program_id(1) * hb
    gcol = gcol_ref[...]
    lane = lax.broadcasted_iota(jnp.int32, gcol.shape, 1)

    def column(i):
        return jnp.sum(jnp.where(lane == i, gcol, 0.0), axis=1, keepdims=True)

    chains = [(d, hh) for d in (0, 1) for hh in range(hb)]
    idx = [d * nh + head0 + hh for d, hh in chains]
    gc = jnp.stack([column(i) for i in idx])
    beta = jnp.stack([column(2 * nh + i) for i in idx])
    total = jnp.stack([column(4 * nh + i) for i in idx])
    gr = jnp.stack([grow_ref[pl.ds(i, 1), :] for i in idx])
    incl_f, strict_f, ri, ci = _chunk_masks(g_rows, DN_CHUNK, False)
    incl_b, strict_b, _, _ = _chunk_masks(g_rows, DN_CHUNK, True)
    incl = jnp.stack([incl_b if d else incl_f for d, _ in chains])
    strict = jnp.stack([strict_b if d else strict_f for d, _ in chains])
    sh = DN_DIAG.bit_length() - 1
    diag = lax.shift_right_logical(ri, sh) == lax.shift_right_logical(ci, sh)

    q = jnp.concatenate([q_ref[...], q_ref[...]], axis=0)
    k = jnp.concatenate([k_ref[...], k_ref[...]], axis=0)
    v = jnp.concatenate([v_ref[...], v_ref[...]], axis=0)
    decay = jnp.exp(jnp.where(incl, gc - gr, -jnp.inf))
    kbeta = k * beta
    k16 = k.astype(MXU_DTYPE)
    lmat = jnp.where(strict, _bdot_nt(kbeta.astype(MXU_DTYPE), k16) * decay, 0.0)
    qk = _bdot_nt(q.astype(MXU_DTYPE), k16) * decay
    t_off = _unit_triangular_inverse_minus_eye(lmat, diag)
    eg = jnp.exp(gc)
    rhs = jnp.concatenate([v * beta, kbeta * eg], axis=2)
    sol = rhs + _bdot(t_off.astype(MXU_DTYPE), rhs.astype(MXU_DTYPE))
    u_ref[...] = sol[:, :, :HEAD_DIM].reshape(u_ref.shape)
    w16 = sol[:, :, HEAD_DIM:].astype(MXU_DTYPE)
    qd16 = (q * eg).astype(MXU_DTYPE)
    kd = k * jnp.exp(total - gc)
    kdt = jnp.stack([kd[c].T for c in range(len(chains))])
    n_c = len(chains)
    for cc in range(n_chunks):
        lo, hi = cc * DN_CHUNK, (cc + 1) * DN_CHUNK
        wq_ref[:, :, cc] = jnp.concatenate([w16[:, lo:hi], qd16[:, lo:hi]], axis=1).reshape(
            (2, hb, 2 * DN_CHUNK, HEAD_DIM))
        tl = slice((cc // 2) * HEAD_DIM, (cc // 2 + 1) * HEAD_DIM)
        both = jnp.concatenate([kdt[:, :, tl], qk[:, lo:hi, tl]], axis=1)
        if cc % 2:
            both = pltpu.roll(both.reshape(n_c * (HEAD_DIM + DN_CHUNK), HEAD_DIM), DN_CHUNK, 1).reshape(both.shape)
        kq_ref[:, :, cc] = both.astype(MXU_DTYPE).reshape((2, hb, HEAD_DIM + DN_CHUNK, HEAD_DIM))
    gts = [jnp.broadcast_to(jnp.exp(total[:, cc * DN_CHUNK:cc * DN_CHUNK + 1, :]), (n_c, 1, HEAD_DIM))
           for cc in range(n_chunks)]
    gts.append(jnp.zeros((n_c, 8 - n_chunks, HEAD_DIM), F32))
    gt_ref[...] = jnp.concatenate(gts, axis=1).reshape(gt_ref.shape)


def _dn_chunks(qkv, gcol, grow, nh):
    b, _, s, _ = qkv.shape
    lanes = gcol.shape[2]
    hb = min(DN_PREP_HEADS, nh)
    assert nh % hb == 0 and DN_GROUP // DN_CHUNK <= 8 and HEAD_DIM == 2 * DN_CHUNK
    nhb, nt, ncg = nh // hb, s // DN_GROUP, DN_GROUP // DN_CHUNK

    def qkv_spec(part):
        return pl.BlockSpec((None, hb, DN_GROUP, HEAD_DIM), lambda i, h, t: (i, part * nhb + h, t, 0))

    return pl.pallas_call(
        functools.partial(_dn_chunk_kernel, nh=nh), name="dn_chunks",
        grid=(b, nhb, nt),
        in_specs=[qkv_spec(0), qkv_spec(1), qkv_spec(2),
                  pl.BlockSpec((None, DN_GROUP, lanes), lambda i, h, t: (i, t, 0)),
                  pl.BlockSpec((None, lanes, DN_GROUP), lambda i, h, t: (i, 0, t))],
        out_specs=[pl.BlockSpec((None, 2, hb, DN_GROUP, HEAD_DIM), lambda i, h, t: (i, 0, h, t, 0)),
                   pl.BlockSpec((None, 2, hb, ncg, 2 * DN_CHUNK, HEAD_DIM), lambda i, h, t: (i, 0, h, t, 0, 0)),
                   pl.BlockSpec((None, 2, hb, ncg, HEAD_DIM + DN_CHUNK, HEAD_DIM),
                                lambda i, h, t: (i, 0, h, t, 0, 0)),
                   pl.BlockSpec((None, 2, hb, None, 8, HEAD_DIM), lambda i, h, t: (i, 0, h, t, 0, 0))],
        out_shape=[jax.ShapeDtypeStruct((b, 2, nh, s, HEAD_DIM), F32),
                   jax.ShapeDtypeStruct((b, 2, nh, s // DN_CHUNK, 2 * DN_CHUNK, HEAD_DIM), MXU_DTYPE),
                   jax.ShapeDtypeStruct((b, 2, nh, s // DN_CHUNK, HEAD_DIM + DN_CHUNK, HEAD_DIM), MXU_DTYPE),
                   jax.ShapeDtypeStruct((b, 2, nh, nt, 8, HEAD_DIM), F32)],
        compiler_params=_cparams("parallel", "parallel", "parallel"),
    )(qkv, qkv, qkv, gcol, grow)


def _dn_scan_kernel(uf, wqf, kqf, gtf, ub, wqb, kqb, gtb, of_ref, ob_ref, state_ref):
    hs = uf.shape[0]
    n_chunks = uf.shape[1] // DN_CHUNK

    @pl.when(pl.program_id(2) == 0)
    def _():
        state_ref[...] = jnp.zeros_like(state_ref)

    state = state_ref[...]
    zeros = jnp.zeros((2 * hs, HEAD_DIM - DN_CHUNK, HEAD_DIM), F32)
    for i in range(n_chunks):
        cf, cb = i, n_chunks - 1 - i
        lo_f, lo_b = cf * DN_CHUNK, cb * DN_CHUNK
        wq = jnp.concatenate([wqf[:, cf], wqb[:, cb]], axis=0)
        kq = jnp.concatenate([kqf[:, cf], kqb[:, cb]], axis=0)
        u = jnp.concatenate([uf[:, lo_f:lo_f + DN_CHUNK], ub[:, lo_b:lo_b + DN_CHUNK]], axis=0)
        gt = jnp.concatenate([gtf[:, cf:cf + 1], gtb[:, cb:cb + 1]], axis=0)
        a1 = _bdot(wq, state.astype(MXU_DTYPE))
        v_new = u - a1[:, :DN_CHUNK]
        v_pad = jnp.concatenate([v_new, zeros], axis=1).astype(MXU_DTYPE)
        a2 = _bdot(kq, v_pad)
        state = state * gt + a2[:, :HEAD_DIM]
        o = a1[:, DN_CHUNK:] + a2[:, HEAD_DIM:]
        of_ref[:, lo_f:lo_f + DN_CHUNK, :] = o[:hs]
        ob_ref[:, lo_b:lo_b + DN_CHUNK, :] = o[hs:]
    state_ref[...] = state


def _dn_scan(u, wq, kq, gt, n_ctx):
    b, _, nh, s, _ = u.shape
    assert n_ctx % DN_GROUP == 0 and s % DN_GROUP == 0
    nc, nt = n_ctx // DN_GROUP, s // DN_GROUP
    n = s - n_ctx
    hs = min(DN_SCAN_HEADS, nh)
    assert nh % hs == 0
    ncg = DN_GROUP // DN_CHUNK

    def fwd(t):
        return t

    def bwd(t):
        return jnp.where(t < nc, nc - 1 - t, nc + nt - 1 - t)

    def specs(d, pos):
        return [pl.BlockSpec((None, None, hs, DN_GROUP, HEAD_DIM), lambda i, h, t: (i, d, h, pos(t), 0)),
                pl.BlockSpec((None, None, hs, ncg, 2 * DN_CHUNK, HEAD_DIM), lambda i, h, t: (i, d, h, pos(t), 0, 0)),
                pl.BlockSpec((None, None, hs, ncg, HEAD_DIM + DN_CHUNK, HEAD_DIM),
                             lambda i, h, t: (i, d, h, pos(t), 0, 0)),
                pl.BlockSpec((None, None, hs, None, 8, HEAD_DIM), lambda i, h, t: (i, d, h, pos(t), 0, 0))]

    def out_spec(first, pos):
        return pl.BlockSpec((None, hs, DN_GROUP, HEAD_DIM),
                            lambda i, h, t: (i, h, jnp.where(t < nc, first, pos(t) - nc), 0))

    out = jax.ShapeDtypeStruct((b, nh, n, HEAD_DIM), F32)
    return pl.pallas_call(
        _dn_scan_kernel, name="dn_scan",
        grid=(b, nh // hs, nt),
        in_specs=specs(0, fwd) + specs(1, bwd),
        out_specs=[out_spec(0, fwd), out_spec(nt - 1 - nc, bwd)],
        out_shape=[out, out],
        scratch_shapes=[pltpu.VMEM((2 * hs, HEAD_DIM, HEAD_DIM), F32)],
        compiler_params=_cparams("parallel", "parallel", "arbitrary"),
    )(u, wq, kq, gt, u, wq, kq, gt)


def _gated_norm_kernel(of_ref, ob_ref, z_ref, w_ref, o_ref):
    w = w_ref[...]
    for h in range(of_ref.shape[0]):
        sl = slice(h * HEAD_DIM, (h + 1) * HEAD_DIM)
        o = of_ref[h] + ob_ref[h]
        z = z_ref[:, sl]
        y = o * lax.rsqrt(jnp.mean(o * o, axis=-1, keepdims=True) + EPS) * w * (z * jax.nn.sigmoid(z))
        o_ref[:, sl] = y.astype(o_ref.dtype)


def _gated_norm(o_f, o_b, p, z_col0, n_ctx, o_norm):
    b, nh, n, _ = o_f.shape
    hpb = nh // 2
    zw = hpb * HEAD_DIM
    assert nh % 2 == 0 and z_col0 % zw == 0
    tr = _tile(n_ctx, 256, 8)
    assert n % tr == 0
    roff = n_ctx // tr
    return pl.pallas_call(
        _gated_norm_kernel, name="gated_norm",
        grid=(b, n // tr, 2),
        in_specs=[pl.BlockSpec((None, hpb, tr, HEAD_DIM), lambda i, r, hf: (i, hf, r, 0)),
                  pl.BlockSpec((None, hpb, tr, HEAD_DIM), lambda i, r, hf: (i, hf, r, 0)),
                  pl.BlockSpec((None, tr, zw), lambda i, r, hf: (i, r + roff, z_col0 // zw + hf)),
                  pl.BlockSpec((1, HEAD_DIM), lambda i, r, hf: (0, 0))],
        out_specs=pl.BlockSpec((None, tr, zw), lambda i, r, hf: (i, r, hf)),
        out_shape=jax.ShapeDtypeStruct((b, n, nh * HEAD_DIM), MXU_DTYPE),
        compiler_params=_cparams("parallel", "parallel", "parallel"),
    )(o_f, o_b, p, o_norm.reshape(1, HEAD_DIM))


def _ffn(x, b, n, norm_g, mod3, w_gate, w_up, w_down):
    d = x.shape[1]
    f = w_gate.shape[1]
    fp = -(-f // FFN_ALIGN) * FFN_ALIGN
    wg = jnp.pad(w_gate.astype(MXU_DTYPE), ((0, 0), (0, fp - f)))
    wu = jnp.pad(w_up.astype(MXU_DTYPE), ((0, 0), (0, fp - f)))
    wd = jnp.pad(w_down.astype(MXU_DTYPE), ((0, fp - f), (0, 0)))
    h = _modnorm(x.reshape(b, n, d), norm_g, mod3, 3, 4, MXU_DTYPE)
    hid = _ffn_up(h, wg, wu)
    return _matmul_residual(hid, wd, x, mod3, 5, n, 2816)


def _hybrid_layer(x, ctx, c, c_ctx, w_mod, b_mod, norm_mix, norm_ffn, w_in, w_out, q_norm, k_norm,
                  conv_w, a_log, dt_bias, o_norm, w_gate, w_up, w_down, layer):
    b, n, d = x.shape
    n_ctx = ctx.shape[1]
    s = n_ctx + n
    wq = d // 2
    wk = wq // GQA_GROUP
    nh = wq // HEAD_DIM
    main = (wq + 2 * wk) + 4 * wq
    assert w_in.shape[1] == main + 4 * nh

    cond8 = jnp.zeros((8, d), F32).at[:b].set(c).at[b].set(c_ctx)
    mod3 = _mod_vectors(cond8, w_mod, b_mod, layer)
    h_all = _modnorm_seq(x, ctx, norm_mix, mod3, b, 0, 1).reshape(b * s, d)

    w_main = w_in[:, :main].astype(MXU_DTYPE)
    w_gates = jnp.pad(w_in[:, main:].astype(MXU_DTYPE), ((0, 0), (0, HEAD_DIM - 4 * nh)))
    p = _matmul(h_all, w_main, F32, 768, 1024).reshape(b, s, main)
    raw_gates = _matmul(h_all, w_gates, F32, 768, HEAD_DIM).reshape(b, s, HEAD_DIM)

    cos, sin = _rope_tables(n_ctx, n)
    qa, kta, va = _qk_prep(p, wq, wk, cos, sin, q_norm, k_norm)
    ya = _attention(qa, kta, va, n_ctx)

    qkv = _dn_prep(p, conv_w, wq + 2 * wk, n_ctx, nh)
    gcol, grow = _dn_gates(raw_gates, a_log, dt_bias, nh)
    u, wqc, kqc, gt = _dn_chunks(qkv, gcol, grow, nh)
    o_f, o_b = _dn_scan(u, wqc, kqc, gt, n_ctx)
    yb = _gated_norm(o_f, o_b, p, wq + 2 * wk + 3 * wq, n_ctx, o_norm)

    x1 = _out_proj_residual(ya.reshape(b * n, wq), yb.reshape(b * n, wq), w_out.astype(MXU_DTYPE),
                            x.reshape(b * n, d), mod3, 2, n)
    return _ffn(x1, b, n, norm_ffn, mod3, w_gate, w_up, w_down)


def _pool_layer(x, b, n, c, w_mod, b_mod, norm_mix, norm_ffn, w_pool, pool_scale, w_gate, w_up, w_down, layer):
    d = x.shape[1]
    cond8 = jnp.zeros((8, d), F32).at[:b].set(c)
    mod3 = _mod_vectors(cond8, w_mod, b_mod, layer)
    h = _modnorm(x.reshape(b, n, d), norm_mix, mod3, 0, 1, F32)
    pooled = _pool(h, b, n)
    x1 = _pool_proj_residual(pooled, w_pool.astype(MXU_DTYPE), pool_scale, x, mod3, 2, n)
    return _ffn(x1, b, n, norm_ffn, mod3, w_gate, w_up, w_down)


def kernel(x, c, ctx, c_ctx, w_mod, b_mod, norm_mix, norm_ffn, w_in, w_out, q_norm, k_norm, conv_w, a_log,
           dt_bias, o_norm, w_pool, pool_scale, w_gate, w_up, w_down):
    b, n, d = x.shape
    depth = w_mod.shape[0]
    assert depth == 2
    xf = _hybrid_layer(x, ctx, c, c_ctx, w_mod, b_mod, norm_mix[0], norm_ffn[0], w_in[0], w_out[0], q_norm[0],
                       k_norm[0], conv_w[0], a_log[0], dt_bias[0], o_norm[0], w_gate[0], w_up[0], w_down[0], 0)
    xf = _pool_layer(xf, b, n, c, w_mod, b_mod, norm_mix[1], norm_ffn[1], w_pool[0], pool_scale[0],
                     w_gate[1], w_up[1], w_down[1], 1)
    return xf.reshape(b, n, d)
```
